```python
import jax, jax.numpy as jnp
from jax import lax
import numpy as np

D_MODEL = 1024
BATCH = 16
SEQ = 256
DEPTH = 2
DEC_BATCH = 2
DEC_SEQ = 1024
PAST_LEN = 256

GRID_W = 64
D_FOURIER = 256
N_FOURIER_GROUPS = 4
FOURIER_GROUP_DIM = D_FOURIER // N_FOURIER_GROUPS
NA_HEADS = 8
NA_HEAD_DIM = 64
D_NA = NA_HEADS * NA_HEAD_DIM
NA_MAX_ROWS = 8
NA_COLS = 16
NA_COL_BLOCK = 16
NA_BAND = 32
POOL_WINDOWS = (2, 4, 8, 16)
N_POOL_GROUPS = 4
D_POOL = 256
POOL_GROUP_DIM = D_POOL // N_POOL_GROUPS
D_IN = D_FOURIER + 3 * D_NA + D_POOL
IN_SPLITS = (D_FOURIER, D_FOURIER + D_NA, D_FOURIER + 2 * D_NA, D_FOURIER + 3 * D_NA)
N_BRANCHES = 3
PEER_HEADS = 8
PEER_N_KEYS = 128
PEER_N_EXPERTS = PEER_N_KEYS * PEER_N_KEYS
PEER_D_KEY = 256
PEER_TOPK = 16
TOKEN_BLOCK = 128
DENSE_BLOCK_THRESHOLD = 2048
EPS = 1e-6
NEG_INF = -1e30

kernel_name = "hybrid_dit_fnet_natten_pool_peer_step"


def rmsnorm(x, g):
    xf = x.astype(jnp.float32)
    y = xf * lax.rsqrt(jnp.mean(xf * xf, axis=-1, keepdims=True) + EPS)
    return (y * g.astype(jnp.float32)).astype(x.dtype)


def adaln(cvec, w_ada, b_ada):
    m = jax.nn.silu(cvec) @ w_ada + b_ada
    return [t[:, None, :] for t in jnp.split(m, 6, axis=-1)]


def fourier_mix(zf):
    B, L, _ = zf.shape
    z = zf.astype(jnp.float32).reshape(B, L, N_FOURIER_GROUPS, FOURIER_GROUP_DIM)
    r = jnp.fft.fft2(z, axes=(1, 3), norm="ortho").real
    return r.reshape(B, L, D_FOURIER).astype(zf.dtype)


def pool_mix(zp, w_pool_grp, pool_scale):
    B, L, _ = zp.shape
    z = zp.astype(jnp.float32).reshape(B, L, N_POOL_GROUPS, POOL_GROUP_DIM)
    cs = jnp.concatenate([jnp.zeros((B, 1, N_POOL_GROUPS, POOL_GROUP_DIM), jnp.float32),
                          jnp.cumsum(z, axis=1)], axis=1)
    t = jnp.arange(L)[:, None]
    half = jnp.array([w // 2 for w in POOL_WINDOWS], dtype=jnp.int32)[None, :]
    lo = jnp.clip(t - half, 0, L)
    hi = jnp.clip(t + half, 0, L)
    gi = jnp.arange(N_POOL_GROUPS)[None, :]
    win_sum = cs[:, hi, gi] - cs[:, lo, gi]
    mean = win_sum / (hi - lo).astype(jnp.float32)[None, :, :, None]
    d = (mean - z).astype(zp.dtype)
    y = jnp.einsum('blgc,gce->blge', d, w_pool_grp) * pool_scale.reshape(N_POOL_GROUPS, POOL_GROUP_DIM)
    return y.reshape(B, L, D_POOL)


def _attend(q, k, v):
    s = jnp.einsum('bqhd,bhkd->bhqk', q, k).astype(jnp.float32)
    p = jax.nn.softmax(s, axis=-1).astype(v.dtype)
    return jnp.einsum('bhqk,bhkd->bqhd', p, v)


def dense_attention(q, k, v):
    B, L, H, dh = q.shape
    if k.shape[2] < DENSE_BLOCK_THRESHOLD:
        return _attend(q, k, v)
    qb = q.reshape(B, L // TOKEN_BLOCK, TOKEN_BLOCK, H, dh).transpose(1, 0, 2, 3, 4)
    ob = lax.map(lambda qq: _attend(qq, k, v), qb)
    return ob.transpose(1, 0, 2, 3, 4).reshape(B, L, H, dh)


def neighbourhood_attention(q, k, v, ck, cv, rel_bias):
    B, N, H, dh = q.shape
    rows = N // GRID_W
    wr = min(NA_MAX_ROWS, rows)
    ncb = GRID_W // NA_COL_BLOCK
    r = jnp.arange(rows)
    kr = jnp.clip(r - wr // 2, 0, rows - wr)[:, None] + jnp.arange(wr)[None, :]
    qc = jnp.arange(ncb)[:, None] * NA_COL_BLOCK + jnp.arange(NA_COL_BLOCK)[None, :]
    kcb = (jnp.clip(jnp.arange(ncb) * NA_COL_BLOCK - NA_COL_BLOCK // 2, 0, GRID_W - NA_BAND)[:, None]
           + jnp.arange(NA_BAND)[None, :])
    cstart = jnp.clip(qc - NA_COLS // 2, 0, GRID_W - NA_COLS)
    valid = (kcb[:, None, :] >= cstart[:, :, None]) & (kcb[:, None, :] < cstart[:, :, None] + NA_COLS)
    nl = wr * NA_BAND
    valid = jnp.broadcast_to(valid[:, :, None, :], (ncb, NA_COL_BLOCK, wr, NA_BAND)).reshape(
        ncb, 1, NA_COL_BLOCK, nl)
    dr = kr - r[:, None] + (NA_MAX_ROWS - 1)
    dc = jnp.clip(kcb[:, None, :] - qc[:, :, None], -(NA_COLS - 1), NA_COLS - 1) + (NA_COLS - 1)
    bias = rel_bias[:, dr[:, None, None, :, None], dc[None, :, :, None, :]]
    bias = bias.transpose(1, 2, 0, 3, 4, 5).reshape(rows, ncb, H, NA_COL_BLOCK, nl)
    qg = q.reshape(B, rows, ncb, NA_COL_BLOCK, H, dh)
    ridx = kr[:, None, :, None]
    cidx = kcb[None, :, None, :]
    kg = k.reshape(B, rows, GRID_W, H, dh)[:, ridx, cidx].reshape(B, rows, ncb, nl, H, dh)
    vg = v.reshape(B, rows, GRID_W, H, dh)[:, ridx, cidx].reshape(B, rows, ncb, nl, H, dh)
    s_loc = jnp.einsum('brjqhd,brjkhd->brjhqk', qg, kg).astype(jnp.float32) + bias.astype(jnp.float32)
    s_loc = jnp.where(valid, s_loc, NEG_INF)
    s_ctx = jnp.einsum('brjqhd,bhkd->brjhqk', qg, ck).astype(jnp.float32)
    p = jax.nn.softmax(jnp.concatenate([s_loc, s_ctx], axis=-1), axis=-1).astype(v.dtype)
    o = (jnp.einsum('brjhqk,brjkhd->brjqhd', p[..., :nl], vg)
         + jnp.einsum('brjhqk,bhkd->brjqhd', p[..., nl:], cv))
    return o.reshape(B, N, H * dh)


def peer_ffn(h, peer_wq, peer_keys, peer_u, peer_v):
    B, L, D = h.shape
    q = (h @ peer_wq).reshape(B, L, PEER_HEADS, 2, PEER_D_KEY // 2)
    s = jnp.einsum('blhpc,hpnc->blhpn', q, peer_keys).astype(jnp.float32)
    sv, si = lax.top_k(s, PEER_TOPK)
    cand = (sv[..., 0, :, None] + sv[..., 1, None, :]).reshape(B, L, PEER_HEADS, PEER_TOPK * PEER_TOPK)
    cid = (si[..., 0, :, None] * PEER_N_KEYS + si[..., 1, None, :]).reshape(
        B, L, PEER_HEADS, PEER_TOPK * PEER_TOPK)
    top, pos = lax.top_k(cand, PEER_TOPK)
    eid = jnp.take_along_axis(cid, pos, axis=-1)
    g = jax.nn.softmax(top, axis=-1).astype(h.dtype)
    nb = (B * L) // TOKEN_BLOCK
    hb = h.reshape(nb, TOKEN_BLOCK, D)
    eb = eid.reshape(nb, TOKEN_BLOCK, PEER_HEADS * PEER_TOPK)
    gb = g.reshape(nb, TOKEN_BLOCK, PEER_HEADS * PEER_TOPK)

    def expert_block(args):
        hh, ee, gg = args
        act = jax.nn.gelu(jnp.einsum('tkd,td->tk', peer_u[ee], hh))
        return jnp.einsum('tk,tkd->td', gg * act, peer_v[ee])

    out = lax.map(expert_block, (hb, eb, gb))
    return out.reshape(B, L, D)


def trunk_layer(x, cvec, ctx_kv, w_ada, b_ada, g_attn, g_ffn, w_in, w_fourier, w_pool_grp, pool_scale,
                w_pool, rel_bias, w_attn, w_gate, b_gate, w_out, peer_wq, peer_keys, peer_u, peer_v):
    shift1, scale1, gate1, shift2, scale2, gate2 = adaln(cvec, w_ada, b_ada)
    B, L, _ = x.shape
    h = rmsnorm(x, g_attn) * (1 + scale1) + shift1
    zf, zq, zk, zv, zp = jnp.split(h @ w_in, IN_SPLITS, axis=-1)
    q = zq.reshape(B, L, NA_HEADS, NA_HEAD_DIM) * (NA_HEAD_DIM ** -0.5)
    k = zk.reshape(B, L, NA_HEADS, NA_HEAD_DIM)
    v = zv.reshape(B, L, NA_HEADS, NA_HEAD_DIM)
    if ctx_kv is None:
        kc = k.transpose(0, 2, 1, 3)
        vc = v.transpose(0, 2, 1, 3)
        attn = dense_attention(q, kc, vc).reshape(B, L, D_NA)
        state = (kc, vc)
    else:
        attn = neighbourhood_attention(q, k, v, ctx_kv[0], ctx_kv[1], rel_bias)
        state = None
    branch_f = fourier_mix(zf) @ w_fourier
    branch_a = attn @ w_attn
    branch_p = pool_mix(zp, w_pool_grp, pool_scale) @ w_pool
    gf, ga, gp = jnp.split(jax.nn.sigmoid(h @ w_gate + b_gate), N_BRANCHES, axis=-1)
    x = x + gate1 * ((gf * branch_f + ga * branch_a + gp * branch_p) @ w_out)
    h2 = rmsnorm(x, g_ffn) * (1 + scale2) + shift2
    x = x + gate2 * peer_ffn(h2, peer_wq, peer_keys, peer_u, peer_v)
    return x, state


def setup_inputs(seed: int = 0) -> dict:
    key = jax.random.key(seed)
    ks = jax.random.split(key, 32)

    def nrm(k, shape, scale):
        return jax.random.normal(k, shape, jnp.float32) * scale

    return {
        "x_prompt": nrm(ks[0], (BATCH, SEQ, D_MODEL), 1.0),
        "x_sample": nrm(ks[1], (DEC_BATCH, DEC_SEQ, D_MODEL), 1.0),
        "cache_k": nrm(ks[2], (DEC_BATCH, DEPTH, NA_HEADS, PAST_LEN, NA_HEAD_DIM), 1.0),
        "cache_v": nrm(ks[3], (DEC_BATCH, DEPTH, NA_HEADS, PAST_LEN, NA_HEAD_DIM), 1.0),
        "c": nrm(ks[4], (DEC_BATCH, D_MODEL), 1.0),
        "c_ctx": nrm(ks[5], (D_MODEL,), 1.0),
        "w_ada": nrm(ks[6], (DEPTH, D_MODEL, 6 * D_MODEL), D_MODEL ** -0.5),
        "b_ada": nrm(ks[7], (DEPTH, 6 * D_MODEL), 0.02),
        "g_attn": 1.0 + nrm(ks[8], (DEPTH, D_MODEL), 0.02),
        "g_ffn": 1.0 + nrm(ks[9], (DEPTH, D_MODEL), 0.02),
        "w_in": nrm(ks[10], (DEPTH, D_MODEL, D_IN), D_MODEL ** -0.5),
        "w_fourier": nrm(ks[11], (DEPTH, D_FOURIER, D_MODEL), D_FOURIER ** -0.5),
        "w_pool_grp": nrm(ks[12], (DEPTH, N_POOL_GROUPS, POOL_GROUP_DIM, POOL_GROUP_DIM), POOL_GROUP_DIM ** -0.5),
        "pool_scale": 1.0 + nrm(ks[13], (DEPTH, D_POOL), 0.02),
        "w_pool": nrm(ks[14], (DEPTH, D_POOL, D_MODEL), D_POOL ** -0.5),
        "rel_bias": nrm(ks[15], (DEPTH, NA_HEADS, 2 * NA_MAX_ROWS - 1, 2 * NA_COLS - 1), 0.1),
        "w_attn": nrm(ks[16], (DEPTH, D_NA, D_MODEL), D_NA ** -0.5),
        "w_gate": nrm(ks[17], (DEPTH, D_MODEL, N_BRANCHES * D_MODEL), D_MODEL ** -0.5),
        "b_gate": nrm(ks[18], (DEPTH, N_BRANCHES * D_MODEL), 0.02),
        "w_out": nrm(ks[19], (DEPTH, D_MODEL, D_MODEL), D_MODEL ** -0.5),
        "peer_wq": nrm(ks[20], (DEPTH, D_MODEL, PEER_HEADS * PEER_D_KEY), D_MODEL ** -0.5),
        "peer_keys": nrm(ks[21], (DEPTH, PEER_HEADS, 2, PEER_N_KEYS, PEER_D_KEY // 2), (PEER_D_KEY // 2) ** -0.5),
        "peer_u": nrm(ks[22], (DEPTH, PEER_N_EXPERTS, D_MODEL), D_MODEL ** -0.5),
        "peer_v": nrm(ks[23], (DEPTH, PEER_N_EXPERTS, D_MODEL), PEER_HEADS ** -0.5),
        "g_final": 1.0 + nrm(ks[24], (D_MODEL,), 0.02),
    }


def reference(x_prompt, x_sample, cache_k, cache_v, c, c_ctx, w_ada, b_ada, g_attn, g_ffn, w_in,
              w_fourier, w_pool_grp, pool_scale, w_pool, rel_bias, w_attn, w_gate, b_gate, w_out,
              peer_wq, peer_keys, peer_u, peer_v, g_final):
    cctx = c_ctx[None, :]
    xp = x_prompt
    xs = x_sample
    new_ks = []
    new_vs = []
    for l in range(DEPTH):
        lw = (w_ada[l], b_ada[l], g_attn[l], g_ffn[l], w_in[l], w_fourier[l], w_pool_grp[l],
              pool_scale[l], w_pool[l], rel_bias[l], w_attn[l], w_gate[l], b_gate[l], w_out[l],
              peer_wq[l], peer_keys[l], peer_u[l], peer_v[l])
        xp, (kl, vl) = trunk_layer(xp, cctx, None, *lw)
        new_ks.append(kl)
        new_vs.append(vl)
        xs, _ = trunk_layer(xs, c, (cache_k[:, l], cache_v[:, l]), *lw)
    y_prompt = rmsnorm(xp, g_final)
    y_sample = rmsnorm(xs, g_final)
    new_k = jnp.stack(new_ks, axis=1)
    new_v = jnp.stack(new_vs, axis=1)
    return (y_prompt, y_sample, new_k, new_v)
```

```python
import functools

import numpy as np
import jax
import jax.numpy as jnp
from jax import lax
from jax.experimental import pallas as pl
from jax.experimental.pallas import tpu as pltpu

F32 = jnp.float32
BF16 = jnp.bfloat16

D_MODEL = 1024
BATCH = 16
SEQ = 256
DEPTH = 2
DEC_BATCH = 2
DEC_SEQ = 1024
PAST_LEN = 256
GRID_W = 64
GRID_ROWS = DEC_SEQ // GRID_W
D_FOURIER = 256
FOURIER_GROUP_DIM = 64
NA_HEADS = 8
NA_HEAD_DIM = 64
D_NA = NA_HEADS * NA_HEAD_DIM
NA_ROWS = 8
NA_COLS = 16
POOL_HALVES = (1, 2, 4, 8)
D_POOL = 256
POOL_GROUP_DIM = 64
D_IN = D_FOURIER + 3 * D_NA + D_POOL
Q_OFF = D_FOURIER
K_OFF = D_FOURIER + D_NA
V_OFF = D_FOURIER + 2 * D_NA
P_OFF = D_FOURIER + 3 * D_NA
PEER_HEADS = 8
PEER_N_KEYS = 128
PEER_N_EXPERTS = PEER_N_KEYS * PEER_N_KEYS
PEER_TOPK = 16
EPS = 1e-6
NEG_INF = -1e30
NEG_BIG = -3.0e38

T_PROMPT = BATCH * SEQ
T_SAMPLE = DEC_BATCH * DEC_SEQ
T_ALL = T_PROMPT + T_SAMPLE
N_MOD_ROWS = 8
N_MOD = 6 * D_MODEL

TOKEN_TILE = 256
PEER_TOKEN_TILE = 512
PEER_I_BLOCK = 8
VMEM_LIMIT = 56 * 1024 * 1024


def _cparams(sem):
    return pltpu.CompilerParams(dimension_semantics=sem, vmem_limit_bytes=VMEM_LIMIT)


def _mod_row(i, tile):
    n_prompt = T_PROMPT // tile
    per_batch = DEC_SEQ // tile
    return jnp.where(i < n_prompt, 0, 1 + (i - n_prompt) // per_batch)


def _rms(x, g):
    return x * lax.rsqrt(jnp.mean(x * x, axis=-1, keepdims=True) + EPS) * g


def _split_bf16(x):
    hi = x.astype(BF16)
    lo = (x - hi.astype(F32)).astype(BF16)
    return hi, lo


def _dot(a, b):
    return jnp.dot(a, b, preferred_element_type=F32)


def _dot_nt(a, b):
    return lax.dot_general(a, b, (((1,), (1,)), ((), ())), preferred_element_type=F32)


def _adaln_kernel(c_ref, w_ref, b_ref, o_ref):
    c = c_ref[...]
    s = c * jax.nn.sigmoid(c)
    s_hi, s_lo = _split_bf16(s)
    w_hi, w_lo = _split_bf16(w_ref[0])
    acc = _dot(s_hi, w_hi) + _dot(s_hi, w_lo) + _dot(s_lo, w_hi)
    o_ref[0] = acc + b_ref[0]


def _adaln(cvec, w_ada, b_ada):
    nb = 1536
    return pl.pallas_call(
        _adaln_kernel,
        grid=(DEPTH, N_MOD // nb),
        in_specs=[
            pl.BlockSpec((N_MOD_ROWS, D_MODEL), lambda l, j: (0, 0)),
            pl.BlockSpec((1, D_MODEL, nb), lambda l, j: (l, 0, j)),
            pl.BlockSpec((1, 1, nb), lambda l, j: (l, 0, j)),
        ],
        out_specs=pl.BlockSpec((1, N_MOD_ROWS, nb), lambda l, j: (l, 0, j)),
        out_shape=jax.ShapeDtypeStruct((DEPTH, N_MOD_ROWS, N_MOD), F32),
        compiler_params=_cparams(("arbitrary", "arbitrary")),
        name="adaln",
    )(cvec, w_ada, b_ada.reshape(DEPTH, 1, N_MOD))


def _na_bias_kernel(rb_ref, o_ref):
    g = pl.program_id(0)
    n_dr = 2 * NA_ROWS - 1
    n_dc = 2 * NA_COLS - 1
    qc = lax.broadcasted_iota(jnp.int32, (GRID_W, 128), 0)
    lane = lax.broadcasted_iota(jnp.int32, (GRID_W, 128), 1)
    kc = lane & (GRID_W - 1)
    upper = lane >= GRID_W
    dcc = jnp.clip(kc - qc, -(NA_COLS - 1), NA_COLS - 1) + (NA_COLS - 1)
    cstart = jnp.clip(qc - NA_COLS // 2, 0, GRID_W - NA_COLS)
    valid = (kc >= cstart) & (kc < cstart + NA_COLS)
    base = g * (n_dr * n_dc)
    planes = []
    for dr in range(n_dr):
        acc = jnp.zeros((GRID_W, 128), F32)
        for d in range(n_dc):
            acc = jnp.where(dcc == d, rb_ref[base + dr * n_dc + d], acc)
        planes.append(acc)
    for dr in range(n_dr - 1):
        o_ref[0, dr] = jnp.where(valid, jnp.where(upper, planes[dr + 1], planes[dr]), NEG_INF)


def _na_bias_table(rel_bias):
    n = DEPTH * NA_HEADS
    return pl.pallas_call(
        _na_bias_kernel,
        grid=(n,),
        in_specs=[pl.BlockSpec(memory_space=pltpu.SMEM)],
        out_specs=pl.BlockSpec((1, 2 * NA_ROWS - 2, GRID_W, 128), lambda g: (g, 0, 0, 0)),
        out_shape=jax.ShapeDtypeStruct((n, 2 * NA_ROWS - 2, GRID_W, 128), F32),
        compiler_params=_cparams(("arbitrary",)),
        name="na_bias",
    )(rel_bias.reshape(-1))


def _front_kernel(x_ref, mod_ref, g_ref, w_ref, z_ref):
    mod = mod_ref[0]
    h = _rms(x_ref[...], g_ref[...]) * (1.0 + mod[:, D_MODEL:2 * D_MODEL]) + mod[:, 0:D_MODEL]
    z_ref[...] = _dot(h.astype(BF16), w_ref[...])


def _front(x, mods, layer, g_attn, w_in_bf):
    t = x.shape[0]
    return pl.pallas_call(
        _front_kernel,
        grid=(t // TOKEN_TILE,),
        in_specs=[
            pl.BlockSpec((TOKEN_TILE, D_MODEL), lambda i: (i, 0)),
            pl.BlockSpec((1, 1, N_MOD), lambda i: (layer * N_MOD_ROWS + _mod_row(i, TOKEN_TILE), 0, 0)),
            pl.BlockSpec((1, D_MODEL), lambda i: (0, 0)),
            pl.BlockSpec((D_MODEL, D_IN), lambda i: (0, 0)),
        ],
        out_specs=pl.BlockSpec((TOKEN_TILE, D_IN), lambda i: (i, 0)),
        out_shape=jax.ShapeDtypeStruct((t, D_IN), F32),
        compiler_params=_cparams(("arbitrary",)),
        name="front",
    )(x, mods, g_attn, w_in_bf)


def _fourier(zf, cl_ref, sl_ref, cb_ref, sb_ref):
    zb = zf.astype(BF16)
    zc = _dot(zb, cb_ref[...].astype(BF16))
    zs = _dot(zb, sb_ref[...].astype(BF16))
    return (_dot(cl_ref[...].astype(BF16), zc.astype(BF16))
            - _dot(sl_ref[...].astype(BF16), zs.astype(BF16)))


def _pool(zp, wpg_ref, ps_ref):
    length = zp.shape[0]
    t = lax.broadcasted_iota(jnp.int32, zp.shape, 0)
    group = lax.broadcasted_iota(jnp.int32, zp.shape, 1) // POOL_GROUP_DIM

    def down(x, k):
        return jnp.where(t >= k, pltpu.roll(x, k, 0), 0.0)

    def up(x, k):
        return jnp.where(t < length - k, pltpu.roll(x, length - k, 0), 0.0)

    fwd = [zp]
    bwd = [down(zp, 1)]
    for n in range(3):
        fwd.append(fwd[n] + up(fwd[n], 1 << n))
        bwd.append(bwd[n] + down(bwd[n], 1 << n))
    win = fwd[3] + bwd[3]
    half = jnp.full(zp.shape, POOL_HALVES[3], jnp.int32)
    for n in (2, 1, 0):
        win = jnp.where(group == n, fwd[n] + bwd[n], win)
        half = jnp.where(group == n, POOL_HALVES[n], half)
    cnt = jnp.minimum(t + half, length) - jnp.maximum(t - half, 0)
    d = win / cnt.astype(F32) - zp
    return _dot(d.astype(BF16), wpg_ref[...]) * ps_ref[...]


def _softmax_pv(parts):
    m = parts[0][0].max(axis=-1, keepdims=True)
    for s, _ in parts[1:]:
        m = jnp.maximum(m, s.max(axis=-1, keepdims=True))
    den = 0.0
    acc = 0.0
    for s, v in parts:
        p = jnp.exp(s - m)
        den = den + p.sum(axis=-1, keepdims=True)
        acc = acc + _dot(p.astype(BF16), v)
    return acc / den


def _mix_prompt_kernel(z_ref, cl_ref, sl_ref, cb_ref, sb_ref, wpg_ref, ps_ref, o_ref, ko_ref, vo_ref):
    o_ref[:, 0:D_FOURIER] = _fourier(z_ref[:, 0:D_FOURIER], cl_ref, sl_ref, cb_ref, sb_ref)
    o_ref[:, D_FOURIER + D_NA:D_MODEL] = _pool(z_ref[:, P_OFF:D_IN], wpg_ref, ps_ref)
    for h in range(NA_HEADS):
        lo = h * NA_HEAD_DIM
        q = z_ref[:, Q_OFF + lo:Q_OFF + lo + NA_HEAD_DIM] * (NA_HEAD_DIM ** -0.5)
        k = z_ref[:, K_OFF + lo:K_OFF + lo + NA_HEAD_DIM]
        v = z_ref[:, V_OFF + lo:V_OFF + lo + NA_HEAD_DIM]
        ko_ref[0, h] = k
        vo_ref[0, h] = v
        s = _dot_nt(q.astype(BF16), k.astype(BF16))
        o_ref[:, D_FOURIER + lo:D_FOURIER + lo + NA_HEAD_DIM] = _softmax_pv([(s, v.astype(BF16))])


def _const_spec(shape):
    nd = len(shape)
    return pl.BlockSpec(shape, lambda *_: (0,) * nd)


def _mix_prompt(z, consts, wpg_bd, pool_scale):
    cl, sl, cb, sb = consts
    kv_shape = jax.ShapeDtypeStruct((BATCH, NA_HEADS, SEQ, NA_HEAD_DIM), F32)
    kv_spec = pl.BlockSpec((1, NA_HEADS, SEQ, NA_HEAD_DIM), lambda b: (b, 0, 0, 0))
    return pl.pallas_call(
        _mix_prompt_kernel,
        grid=(BATCH,),
        in_specs=[
            pl.BlockSpec((SEQ, D_IN), lambda b: (b, 0)),
            _const_spec(cl.shape), _const_spec(sl.shape), _const_spec(cb.shape), _const_spec(sb.shape),
            _const_spec(wpg_bd.shape), _const_spec(pool_scale.shape),
        ],
        out_specs=[pl.BlockSpec((SEQ, D_MODEL), lambda b: (b, 0)), kv_spec, kv_spec],
        out_shape=[jax.ShapeDtypeStruct((T_ALL, D_MODEL), F32), kv_shape, kv_shape],
        compiler_params=_cparams(("arbitrary",)),
        name="mix_prompt",
    )(z, cl, sl, cb, sb, wpg_bd, pool_scale)


def _mix_sample_kernel(z_ref, prev_ref, ck_ref, cv_ref, tb_ref, cl_ref, sl_ref, cb_ref, sb_ref, wpg_ref, ps_ref,
                       o_ref, q_sc, k_sc, v_sc, o_sc):
    del prev_ref
    o_ref[:, 0:D_FOURIER] = _fourier(z_ref[:, 0:D_FOURIER], cl_ref, sl_ref, cb_ref, sb_ref)
    o_ref[:, D_FOURIER + D_NA:D_MODEL] = _pool(z_ref[:, P_OFF:D_IN], wpg_ref, ps_ref)
    for h in range(NA_HEADS):
        lo = h * NA_HEAD_DIM
        q_sc[h] = (z_ref[:, Q_OFF + lo:Q_OFF + lo + NA_HEAD_DIM] * (NA_HEAD_DIM ** -0.5)).astype(BF16)
        k_sc[h] = z_ref[:, K_OFF + lo:K_OFF + lo + NA_HEAD_DIM].astype(BF16)
        v_sc[h] = z_ref[:, V_OFF + lo:V_OFF + lo + NA_HEAD_DIM].astype(BF16)

    def head_body(h, carry):
        ckh = ck_ref[0, 0, h].astype(BF16)
        cvh = cv_ref[0, 0, h].astype(BF16)
        for r in range(GRID_ROWS):
            rs = min(max(r - NA_ROWS // 2, 0), GRID_ROWS - NA_ROWS)
            off = rs - r + NA_ROWS - 1
            q = q_sc[h, r * GRID_W:(r + 1) * GRID_W, :]
            kw = k_sc[h, rs * GRID_W:(rs + NA_ROWS) * GRID_W, :]
            vw = v_sc[h, rs * GRID_W:(rs + NA_ROWS) * GRID_W, :]
            bias = jnp.concatenate([tb_ref[h, off + 2 * m] for m in range(NA_ROWS // 2)], axis=-1)
            s_loc = _dot_nt(q, kw) + bias
            s_ctx = _dot_nt(q, ckh)
            o_sc[h, r * GRID_W:(r + 1) * GRID_W, :] = _softmax_pv([(s_loc, vw), (s_ctx, cvh)])
        return carry

    lax.fori_loop(0, NA_HEADS, head_body, 0)
    for h in range(NA_HEADS):
        lo = D_FOURIER + h * NA_HEAD_DIM
        o_ref[:, lo:lo + NA_HEAD_DIM] = o_sc[h]


def _mix_sample(z, bin_prev, cache_k, cache_v, bias_tbl, layer, consts, wpg_bd, pool_scale):
    cl, sl, cb, sb = consts
    first = T_PROMPT // DEC_SEQ
    n_dr = 2 * NA_ROWS - 2
    cache_spec = pl.BlockSpec((1, 1, NA_HEADS, PAST_LEN, NA_HEAD_DIM), lambda b: (b, layer, 0, 0, 0))
    return pl.pallas_call(
        _mix_sample_kernel,
        grid=(DEC_BATCH,),
        in_specs=[
            pl.BlockSpec((DEC_SEQ, D_IN), lambda b: (first + b, 0)),
            pl.BlockSpec(memory_space=pl.ANY),
            cache_spec, cache_spec,
            pl.BlockSpec((NA_HEADS, n_dr, GRID_W, 128), lambda b: (layer, 0, 0, 0)),
            _const_spec(cl.shape), _const_spec(sl.shape), _const_spec(cb.shape), _const_spec(sb.shape),
            _const_spec(wpg_bd.shape), _const_spec(pool_scale.shape),
        ],
        out_specs=pl.BlockSpec((DEC_SEQ, D_MODEL), lambda b: (first + b, 0)),
        out_shape=jax.ShapeDtypeStruct((T_ALL, D_MODEL), F32),
        scratch_shapes=[
            pltpu.VMEM((NA_HEADS, DEC_SEQ, NA_HEAD_DIM), BF16),
            pltpu.VMEM((NA_HEADS, DEC_SEQ, NA_HEAD_DIM), BF16),
            pltpu.VMEM((NA_HEADS, DEC_SEQ, NA_HEAD_DIM), BF16),
            pltpu.VMEM((NA_HEADS, DEC_SEQ, NA_HEAD_DIM), F32),
        ],
        input_output_aliases={1: 0},
        compiler_params=_cparams(("arbitrary",)),
        name="mix_sample",
    )(z, bin_prev, cache_k, cache_v, bias_tbl, cl, sl, cb, sb, wpg_bd, pool_scale)


def _merge_kernel(x_ref, b_ref, mod_ref, ga_ref, gf_ref, wg_ref, bg_ref, wf_ref, wa_ref, wp_ref, wo_ref,
                  xo_ref, h2t_ref):
    x = x_ref[...]
    mod = mod_ref[0]
    h = _rms(x, ga_ref[...]) * (1.0 + mod[:, D_MODEL:2 * D_MODEL]) + mod[:, 0:D_MODEL]
    gates = jax.nn.sigmoid(_dot(h.astype(BF16), wg_ref[...]) + bg_ref[...])
    bf = _dot(b_ref[:, 0:D_FOURIER].astype(BF16), wf_ref[...])
    ba = _dot(b_ref[:, D_FOURIER:D_FOURIER + D_NA].astype(BF16), wa_ref[...])
    bp = _dot(b_ref[:, D_FOURIER + D_NA:D_MODEL].astype(BF16), wp_ref[...])
    merged = (gates[:, 0:D_MODEL] * bf + gates[:, D_MODEL:2 * D_MODEL] * ba
              + gates[:, 2 * D_MODEL:3 * D_MODEL] * bp)
    xn = x + mod[:, 2 * D_MODEL:3 * D_MODEL] * _dot(merged.astype(BF16), wo_ref[...])
    xo_ref[...] = xn
    h2 = _rms(xn, gf_ref[...]) * (1.0 + mod[:, 4 * D_MODEL:5 * D_MODEL]) + mod[:, 3 * D_MODEL:4 * D_MODEL]
    h2t_ref[...] = h2.T.astype(BF16)


def _merge(x, bins, mods, layer, g_attn, g_ffn, wg, bg, wf, wa, wp, wo):
    t = x.shape[0]
    row = pl.BlockSpec((TOKEN_TILE, D_MODEL), lambda i: (i, 0))
    return pl.pallas_call(
        _merge_kernel,
        grid=(t // TOKEN_TILE,),
        in_specs=[
            row, row,
            pl.BlockSpec((1, 1, N_MOD), lambda i: (layer * N_MOD_ROWS + _mod_row(i, TOKEN_TILE), 0, 0)),
            _const_spec(g_attn.shape), _const_spec(g_ffn.shape),
            _const_spec(wg.shape), _const_spec(bg.shape),
            _const_spec(wf.shape), _const_spec(wa.shape), _const_spec(wp.shape), _const_spec(wo.shape),
        ],
        out_specs=[row, pl.BlockSpec((D_MODEL, TOKEN_TILE), lambda i: (0, i))],
        out_shape=[jax.ShapeDtypeStruct((t, D_MODEL), F32), jax.ShapeDtypeStruct((D_MODEL, t), BF16)],
        compiler_params=_cparams(("arbitrary",)),
        name="merge",
    )(x, bins, mods, g_attn, g_ffn, wg, bg, wf, wa, wp, wo)


def _extract_sorted(s, n, out_ref):
    for r in range(n):
        m = s.max(axis=0, keepdims=True)
        out_ref[r:r + 1, :] = m
        s = jnp.where(s == m, NEG_BIG, s)


def _peer_front_kernel(h2t_ref, wqt_ref, keys_ref, thr_ref, e1_ref, s2_ref, e2_ref, s_sc, a_sc, b_sc, c_sc):
    tt = h2t_ref.shape[1]
    nch = tt // 128
    h2t = h2t_ref[...]
    for hp in range(2 * PEER_HEADS):
        qp = _dot(wqt_ref[hp * 128:(hp + 1) * 128, :], h2t)
        s_sc[hp] = _dot(keys_ref[hp], qp.astype(BF16))

    def body(idx, carry):
        h = idx // nch
        col = pl.multiple_of((idx % nch) * 128, 128)
        s1 = s_sc[2 * h, :, pl.ds(col, 128)]
        s2 = s_sc[2 * h + 1, :, pl.ds(col, 128)]
        _extract_sorted(s1, PEER_TOPK + 1, a_sc)
        _extract_sorted(s2, PEER_TOPK + 1, b_sc)
        off = 0
        for r in range(PEER_TOPK):
            n = PEER_TOPK // (r + 1)
            c_sc[off:off + n, :] = a_sc[r:r + 1, :] + b_sc[0:n, :]
            off += n
        c_sc[off:off + 1, :] = a_sc[0:1, :] + b_sc[PEER_TOPK:PEER_TOPK + 1, :]
        c_sc[off + 1:off + 2, :] = a_sc[PEER_TOPK:PEER_TOPK + 1, :] + b_sc[0:1, :]
        c_sc[off + 2:, :] = jnp.full((c_sc.shape[0] - off - 2, 128), NEG_BIG, F32)
        c = c_sc[...]
        vals = []
        for r in range(PEER_TOPK + 1):
            m = c.max(axis=0, keepdims=True)
            vals.append(m)
            c = jnp.where(c == m, NEG_BIG, c)
        z = jnp.exp(vals[0] - vals[0])
        for r in range(1, PEER_TOPK):
            z = z + jnp.exp(vals[r] - vals[0])
        tau = 0.5 * (vals[PEER_TOPK - 1] + vals[PEER_TOPK])
        thr_ref[h, :, pl.ds(col, 128)] = tau - s1
        e1_ref[h, :, pl.ds(col, 128)] = jnp.exp(s1 - a_sc[0:1, :])
        s2_ref[h, :, pl.ds(col, 128)] = s2
        e2_ref[h, :, pl.ds(col, 128)] = jnp.exp(s2 - b_sc[0:1, :]) / z
        return carry

    lax.fori_loop(0, PEER_HEADS * nch, body, 0)


def _peer_front(h2t, wqt, keys):
    t = h2t.shape[1]
    tt = PEER_TOKEN_TILE
    tbl = jax.ShapeDtypeStruct((PEER_HEADS, PEER_N_KEYS, t), F32)
    tbl_spec = pl.BlockSpec((PEER_HEADS, PEER_N_KEYS, tt), lambda i: (0, 0, i))
    return pl.pallas_call(
        _peer_front_kernel,
        grid=(t // tt,),
        in_specs=[
            pl.BlockSpec((D_MODEL, tt), lambda i: (0, i)),
            _const_spec(wqt.shape), _const_spec(keys.shape),
        ],
        out_specs=[tbl_spec] * 4,
        out_shape=[tbl] * 4,
        scratch_shapes=[
            pltpu.VMEM((2 * PEER_HEADS, PEER_N_KEYS, tt), F32),
            pltpu.VMEM((24, 128), F32),
            pltpu.VMEM((24, 128), F32),
            pltpu.VMEM((56, 128), F32),
        ],
        compiler_params=_cparams(("arbitrary",)),
        name="peer_front",
    )(h2t, wqt, keys)


def _peer_dense_kernel(h2t_ref, u_ref, v_ref, thr_ref, e1_ref, s2_ref, e2_ref, x_ref, mod_ref, o_ref,
                       acc_sc, at_sc, gt_sc):
    e = pl.program_id(1)
    tt = h2t_ref.shape[1]
    nch = tt // 128

    @pl.when(e == 0)
    def _():
        acc_sc[...] = jnp.zeros_like(acc_sc)

    at_sc[...] = _dot(u_ref[...], h2t_ref[...])

    def body(ii, c, carry):
        col = pl.multiple_of(c * 128, 128)
        w = jnp.zeros((PEER_N_KEYS, 128), F32)
        for h in range(PEER_HEADS):
            thr = thr_ref[h, ii:ii + 1, pl.ds(col, 128)]
            e1 = e1_ref[h, ii:ii + 1, pl.ds(col, 128)]
            s2 = s2_ref[h, :, pl.ds(col, 128)]
            e2 = e2_ref[h, :, pl.ds(col, 128)]
            w = w + jnp.where(s2 >= thr, e2, 0.0) * e1
        g = jax.nn.gelu(at_sc[ii * 128:(ii + 1) * 128, pl.ds(col, 128)]) * w
        gt_sc[ii * 128:(ii + 1) * 128, pl.ds(col, 128)] = g.astype(BF16)
        return carry

    for ii in range(PEER_I_BLOCK):
        lax.fori_loop(0, nch, functools.partial(body, ii), 0)
    acc_sc[...] += lax.dot_general(gt_sc[...], v_ref[...], (((0,), (0,)), ((), ())),
                                   preferred_element_type=F32)

    @pl.when(e == pl.num_programs(1) - 1)
    def _():
        o_ref[...] = x_ref[...] + mod_ref[0][:, 5 * D_MODEL:6 * D_MODEL] * acc_sc[...]


def _peer_dense(h2t, u_bf, v_bf, tables, x, mods, layer):
    thr, e1, s2, e2 = tables
    t = x.shape[0]
    tt = PEER_TOKEN_TILE
    eb = PEER_I_BLOCK * PEER_N_KEYS
    i_spec = pl.BlockSpec((PEER_HEADS, PEER_I_BLOCK, tt), lambda i, e: (0, e, i))
    j_spec = pl.BlockSpec((PEER_HEADS, PEER_N_KEYS, tt), lambda i, e: (0, 0, i))
    row = pl.BlockSpec((tt, D_MODEL), lambda i, e: (i, 0))
    return pl.pallas_call(
        _peer_dense_kernel,
        grid=(t // tt, PEER_N_EXPERTS // eb),
        in_specs=[
            pl.BlockSpec((D_MODEL, tt), lambda i, e: (0, i)),
            pl.BlockSpec((eb, D_MODEL), lambda i, e: (e, 0)),
            pl.BlockSpec((eb, D_MODEL), lambda i, e: (e, 0)),
            i_spec, i_spec, j_spec, j_spec,
            row,
            pl.BlockSpec((1, 1, N_MOD), lambda i, e: (layer * N_MOD_ROWS + _mod_row(i, tt), 0, 0)),
        ],
        out_specs=row,
        out_shape=jax.ShapeDtypeStruct((t, D_MODEL), F32),
        scratch_shapes=[
            pltpu.VMEM((tt, D_MODEL), F32),
            pltpu.VMEM((eb, tt), F32),
            pltpu.VMEM((eb, tt), BF16),
        ],
        compiler_params=_cparams(("arbitrary", "arbitrary")),
        name="peer_dense",
    )(h2t, u_bf, v_bf, thr, e1, s2, e2, x, mods)


def _final_norm_kernel(x_ref, g_ref, o_ref):
    o_ref[...] = _rms(x_ref[...], g_ref[...])


def _final_norm(x, g):
    t = x.shape[0]
    row = pl.BlockSpec((TOKEN_TILE, D_MODEL), lambda i: (i, 0))
    return pl.pallas_call(
        _final_norm_kernel,
        grid=(t // TOKEN_TILE,),
        in_specs=[row, _const_spec(g.shape)],
        out_specs=row,
        out_shape=jax.ShapeDtypeStruct((t, D_MODEL), F32),
        compiler_params=_cparams(("arbitrary",)),
        name="final_norm",
    )(x, g)


def _dft_consts(length):
    def cs(n):
        k = np.arange(n, dtype=np.float64)
        ang = 2.0 * np.pi * np.outer(k, k) / n
        return np.cos(ang) / np.sqrt(n), np.sin(ang) / np.sqrt(n)

    cl, sl = cs(length)
    cg, sg = cs(FOURIER_GROUP_DIM)
    eye = np.eye(D_FOURIER // FOURIER_GROUP_DIM)
    return tuple(jnp.asarray(m, F32) for m in (cl, sl, np.kron(eye, cg), np.kron(eye, sg)))


def _block_diag(w):
    g, c, e = w.shape
    eye = jnp.eye(g, dtype=w.dtype)
    return (eye[:, None, :, None] * w[:, :, None, :]).reshape(g * c, g * e)


def kernel(x_prompt, x_sample, cache_k, cache_v, c, c_ctx, w_ada, b_ada, g_attn, g_ffn, w_in, w_fourier,
           w_pool_grp, pool_scale, w_pool, rel_bias, w_attn, w_gate, b_gate, w_out, peer_wq, peer_keys,
           peer_u, peer_v, g_final):
    x = jnp.concatenate([x_prompt.reshape(T_PROMPT, D_MODEL), x_sample.reshape(T_SAMPLE, D_MODEL)], axis=0)
    cvec = jnp.zeros((N_MOD_ROWS, D_MODEL), F32).at[0].set(c_ctx).at[1:1 + DEC_BATCH].set(c)
    mods = _adaln(cvec, w_ada, b_ada).reshape(DEPTH * N_MOD_ROWS, 1, N_MOD)
    bias_tbl = _na_bias_table(rel_bias)
    consts_p = _dft_consts(SEQ)
    consts_s = _dft_consts(DEC_SEQ)

    new_k, new_v = [], []
    for l in range(DEPTH):
        ga = g_attn[l].reshape(1, D_MODEL)
        gf = g_ffn[l].reshape(1, D_MODEL)
        wpg_bd = _block_diag(w_pool_grp[l]).astype(BF16)
        ps = pool_scale[l].reshape(1, D_POOL)
        z = _front(x, mods, l, ga, w_in[l].astype(BF16))
        bins, kl, vl = _mix_prompt(z, consts_p, wpg_bd, ps)
        bins = _mix_sample(z, bins, cache_k, cache_v, bias_tbl, l, consts_s, wpg_bd, ps)
        new_k.append(kl)
        new_v.append(vl)
        x, h2t = _merge(x, bins, mods, l, ga, gf, w_gate[l].astype(BF16), b_gate[l].reshape(1, -1),
                        w_fourier[l].astype(BF16), w_attn[l].astype(BF16), w_pool[l].astype(BF16),
                        w_out[l].astype(BF16))
        keys = peer_keys[l].reshape(2 * PEER_HEADS, PEER_N_KEYS, -1).astype(BF16)
        tables = _peer_front(h2t, peer_wq[l].T.astype(BF16), keys)
        x = _peer_dense(h2t, peer_u[l].astype(BF16), peer_v[l].astype(BF16), tables, x, mods, l)

    y = _final_norm(x, g_final.reshape(1, D_MODEL))
    y_prompt = y[:T_PROMPT].reshape(BATCH, SEQ, D_MODEL)
    y_sample = y[T_PROMPT:].reshape(DEC_BATCH, DEC_SEQ, D_MODEL)
    return y_prompt, y_sample, jnp.stack(new_k, axis=1), jnp.stack(new_v, axis=1)
```

```python
import functools

import numpy as np
import jax
import jax.numpy as jnp
from jax import lax
from jax.experimental import pallas as pl
from jax.experimental.pallas import tpu as pltpu

F32 = jnp.float32
BF16 = jnp.bfloat16

D_MODEL = 1024
BATCH = 16
SEQ = 256
DEPTH = 2
DEC_BATCH = 2
DEC_SEQ = 1024
PAST_LEN = 256
GRID_W = 64
GRID_ROWS = DEC_SEQ // GRID_W
D_FOURIER = 256
FOURIER_GROUP_DIM = 64
NA_HEADS = 8
NA_HEAD_DIM = 64
D_NA = NA_HEADS * NA_HEAD_DIM
NA_ROWS = 8
NA_COLS = 16
POOL_HALVES = (1, 2, 4, 8)
D_POOL = 256
POOL_GROUP_DIM = 64
D_IN = D_FOURIER + 3 * D_NA + D_POOL
Q_OFF = D_FOURIER
K_OFF = D_FOURIER + D_NA
V_OFF = D_FOURIER + 2 * D_NA
P_OFF = D_FOURIER + 3 * D_NA
PEER_HEADS = 8
PEER_N_KEYS = 128
PEER_N_EXPERTS = PEER_N_KEYS * PEER_N_KEYS
PEER_TOPK = 16
EPS = 1e-6
NEG_INF = -1e30
NEG_BIG = -3.0e38

T_PROMPT = BATCH * SEQ
T_SAMPLE = DEC_BATCH * DEC_SEQ
T_ALL = T_PROMPT + T_SAMPLE
N_MOD_ROWS = 8
N_MOD = 6 * D_MODEL

TOKEN_TILE = 256
PEER_TOKEN_TILE = 512
PEER_I_BLOCK = 8
PEER_N_BLOCKS = PEER_N_KEYS // PEER_I_BLOCK
PEER_STEPS_PER_TILE = PEER_N_BLOCKS // 2
VMEM_LIMIT = 56 * 1024 * 1024


def _cparams(sem):
    return pltpu.CompilerParams(dimension_semantics=sem, vmem_limit_bytes=VMEM_LIMIT)


def _mod_row(i, tile):
    n_prompt = T_PROMPT // tile
    per_batch = DEC_SEQ // tile
    return jnp.where(i < n_prompt, 0, 1 + (i - n_prompt) // per_batch)


def _rms(x, g):
    return x * lax.rsqrt(jnp.mean(x * x, axis=-1, keepdims=True) + EPS) * g


def _split_bf16(x):
    hi = x.astype(BF16)
    lo = (x - hi.astype(F32)).astype(BF16)
    return hi, lo


def _dot(a, b):
    return jnp.dot(a, b, preferred_element_type=F32)


def _dot_nt(a, b):
    return lax.dot_general(a, b, (((1,), (1,)), ((), ())), preferred_element_type=F32)


def _adaln_kernel(c_ref, w_ref, b_ref, o_ref):
    c = c_ref[...]
    s = c * jax.nn.sigmoid(c)
    s_hi, s_lo = _split_bf16(s)
    w_hi, w_lo = _split_bf16(w_ref[0])
    acc = _dot(s_hi, w_hi) + _dot(s_hi, w_lo) + _dot(s_lo, w_hi)
    o_ref[0] = acc + b_ref[0]


def _adaln(cvec, w_ada, b_ada):
    nb = 1536
    return pl.pallas_call(
        _adaln_kernel,
        grid=(DEPTH, N_MOD // nb),
        in_specs=[
            pl.BlockSpec((N_MOD_ROWS, D_MODEL), lambda l, j: (0, 0)),
            pl.BlockSpec((1, D_MODEL, nb), lambda l, j: (l, 0, j)),
            pl.BlockSpec((1, 1, nb), lambda l, j: (l, 0, j)),
        ],
        out_specs=pl.BlockSpec((1, N_MOD_ROWS, nb), lambda l, j: (l, 0, j)),
        out_shape=jax.ShapeDtypeStruct((DEPTH, N_MOD_ROWS, N_MOD), F32),
        compiler_params=_cparams(("arbitrary", "arbitrary")),
        name="adaln",
    )(cvec, w_ada, b_ada.reshape(DEPTH, 1, N_MOD))


def _na_bias_kernel(rb_ref, o_ref):
    g = pl.program_id(0)
    n_dr = 2 * NA_ROWS - 1
    n_dc = 2 * NA_COLS - 1
    qc = lax.broadcasted_iota(jnp.int32, (GRID_W, 128), 0)
    lane = lax.broadcasted_iota(jnp.int32, (GRID_W, 128), 1)
    kc = lane & (GRID_W - 1)
    upper = lane >= GRID_W
    dcc = jnp.clip(kc - qc, -(NA_COLS - 1), NA_COLS - 1) + (NA_COLS - 1)
    cstart = jnp.clip(qc - NA_COLS // 2, 0, GRID_W - NA_COLS)
    valid = (kc >= cstart) & (kc < cstart + NA_COLS)
    base = g * (n_dr * n_dc)
    planes = []
    for dr in range(n_dr):
        acc = jnp.zeros((GRID_W, 128), F32)
        for d in range(n_dc):
            acc = jnp.where(dcc == d, rb_ref[base + dr * n_dc + d], acc)
        planes.append(acc)
    for dr in range(n_dr - 1):
        o_ref[0, dr] = jnp.where(valid, jnp.where(upper, planes[dr + 1], planes[dr]), NEG_INF)


def _na_bias_table(rel_bias):
    n = DEPTH * NA_HEADS
    return pl.pallas_call(
        _na_bias_kernel,
        grid=(n,),
        in_specs=[pl.BlockSpec(memory_space=pltpu.SMEM)],
        out_specs=pl.BlockSpec((1, 2 * NA_ROWS - 2, GRID_W, 128), lambda g: (g, 0, 0, 0)),
        out_shape=jax.ShapeDtypeStruct((n, 2 * NA_ROWS - 2, GRID_W, 128), F32),
        compiler_params=_cparams(("arbitrary",)),
        name="na_bias",
    )(rel_bias.reshape(-1))


def _front_kernel(x_ref, mod_ref, g_ref, w_ref, z_ref):
    mod = mod_ref[0]
    h = _rms(x_ref[...], g_ref[...]) * (1.0 + mod[:, D_MODEL:2 * D_MODEL]) + mod[:, 0:D_MODEL]
    z_ref[...] = _dot(h.astype(BF16), w_ref[...])


def _front(x, mods, layer, g_attn, w_in_bf):
    t = x.shape[0]
    return pl.pallas_call(
        _front_kernel,
        grid=(t // TOKEN_TILE,),
        in_specs=[
            pl.BlockSpec((TOKEN_TILE, D_MODEL), lambda i: (i, 0)),
            pl.BlockSpec((1, 1, N_MOD), lambda i: (layer * N_MOD_ROWS + _mod_row(i, TOKEN_TILE), 0, 0)),
            pl.BlockSpec((1, D_MODEL), lambda i: (0, 0)),
            pl.BlockSpec((D_MODEL, D_IN), lambda i: (0, 0)),
        ],
        out_specs=pl.BlockSpec((TOKEN_TILE, D_IN), lambda i: (i, 0)),
        out_shape=jax.ShapeDtypeStruct((t, D_IN), F32),
        compiler_params=_cparams(("arbitrary",)),
        name="front",
    )(x, mods, g_attn, w_in_bf)


def _fourier(zf, cl_ref, sl_ref, cb_ref, sb_ref):
    zb = zf.astype(BF16)
    zc = _dot(zb, cb_ref[...].astype(BF16))
    zs = _dot(zb, sb_ref[...].astype(BF16))
    return (_dot(cl_ref[...].astype(BF16), zc.astype(BF16))
            - _dot(sl_ref[...].astype(BF16), zs.astype(BF16)))


def _pool(zp, wpg_ref, ps_ref):
    length = zp.shape[0]
    t = lax.broadcasted_iota(jnp.int32, zp.shape, 0)
    group = lax.broadcasted_iota(jnp.int32, zp.shape, 1) // POOL_GROUP_DIM

    def down(x, k):
        return jnp.where(t >= k, pltpu.roll(x, k, 0), 0.0)

    def up(x, k):
        return jnp.where(t < length - k, pltpu.roll(x, length - k, 0), 0.0)

    fwd = [zp]
    bwd = [down(zp, 1)]
    for n in range(3):
        fwd.append(fwd[n] + up(fwd[n], 1 << n))
        bwd.append(bwd[n] + down(bwd[n], 1 << n))
    win = fwd[3] + bwd[3]
    half = jnp.full(zp.shape, POOL_HALVES[3], jnp.int32)
    for n in (2, 1, 0):
        win = jnp.where(group == n, fwd[n] + bwd[n], win)
        half = jnp.where(group == n, POOL_HALVES[n], half)
    cnt = jnp.minimum(t + half, length) - jnp.maximum(t - half, 0)
    d = win / cnt.astype(F32) - zp
    return _dot(d.astype(BF16), wpg_ref[...]) * ps_ref[...]


def _softmax_pv(parts):
    m = parts[0][0].max(axis=-1, keepdims=True)
    for s, _ in parts[1:]:
        m = jnp.maximum(m, s.max(axis=-1, keepdims=True))
    den = 0.0
    acc = 0.0
    for s, v in parts:
        p = jnp.exp(s - m)
        den = den + p.sum(axis=-1, keepdims=True)
        acc = acc + _dot(p.astype(BF16), v)
    return acc / den


def _mix_prompt_kernel(z_ref, cl_ref, sl_ref, cb_ref, sb_ref, wpg_ref, ps_ref, o_ref, ko_ref, vo_ref):
    o_ref[:, 0:D_FOURIER] = _fourier(z_ref[:, 0:D_FOURIER], cl_ref, sl_ref, cb_ref, sb_ref)
    o_ref[:, D_FOURIER + D_NA:D_MODEL] = _pool(z_ref[:, P_OFF:D_IN], wpg_ref, ps_ref)
    for h in range(NA_HEADS):
        lo = h * NA_HEAD_DIM
        q = z_ref[:, Q_OFF + lo:Q_OFF + lo + NA_HEAD_DIM] * (NA_HEAD_DIM ** -0.5)
        k = z_ref[:, K_OFF + lo:K_OFF + lo + NA_HEAD_DIM]
        v = z_ref[:, V_OFF + lo:V_OFF + lo + NA_HEAD_DIM]
        ko_ref[0, h] = k
        vo_ref[0, h] = v
        s = _dot_nt(q.astype(BF16), k.astype(BF16))
        o_ref[:, D_FOURIER + lo:D_FOURIER + lo + NA_HEAD_DIM] = _softmax_pv([(s, v.astype(BF16))])


def _const_spec(shape):
    nd = len(shape)
    return pl.BlockSpec(shape, lambda *_: (0,) * nd)


def _mix_prompt(z, consts, wpg_bd, pool_scale):
    cl, sl, cb, sb = consts
    kv_shape = jax.ShapeDtypeStruct((BATCH, NA_HEADS, SEQ, NA_HEAD_DIM), F32)
    kv_spec = pl.BlockSpec((1, NA_HEADS, SEQ, NA_HEAD_DIM), lambda b: (b, 0, 0, 0))
    return pl.pallas_call(
        _mix_prompt_kernel,
        grid=(BATCH,),
        in_specs=[
            pl.BlockSpec((SEQ, D_IN), lambda b: (b, 0)),
            _const_spec(cl.shape), _const_spec(sl.shape), _const_spec(cb.shape), _const_spec(sb.shape),
            _const_spec(wpg_bd.shape), _const_spec(pool_scale.shape),
        ],
        out_specs=[pl.BlockSpec((SEQ, D_MODEL), lambda b: (b, 0)), kv_spec, kv_spec],
        out_shape=[jax.ShapeDtypeStruct((T_ALL, D_MODEL), F32), kv_shape, kv_shape],
        compiler_params=_cparams(("arbitrary",)),
        name="mix_prompt",
    )(z, cl, sl, cb, sb, wpg_bd, pool_scale)


def _mix_sample_kernel(z_ref, prev_ref, ck_ref, cv_ref, tb_ref, cl_ref, sl_ref, cb_ref, sb_ref, wpg_ref, ps_ref,
                       o_ref, q_sc, k_sc, v_sc, o_sc):
    del prev_ref
    o_ref[:, 0:D_FOURIER] = _fourier(z_ref[:, 0:D_FOURIER], cl_ref, sl_ref, cb_ref, sb_ref)
    o_ref[:, D_FOURIER + D_NA:D_MODEL] = _pool(z_ref[:, P_OFF:D_IN], wpg_ref, ps_ref)
    for h in range(NA_HEADS):
        lo = h * NA_HEAD_DIM
        q_sc[h] = (z_ref[:, Q_OFF + lo:Q_OFF + lo + NA_HEAD_DIM] * (NA_HEAD_DIM ** -0.5)).astype(BF16)
        k_sc[h] = z_ref[:, K_OFF + lo:K_OFF + lo + NA_HEAD_DIM].astype(BF16)
        v_sc[h] = z_ref[:, V_OFF + lo:V_OFF + lo + NA_HEAD_DIM].astype(BF16)

    def head_body(h, carry):
        ckh = ck_ref[0, 0, h].astype(BF16)
        cvh = cv_ref[0, 0, h].astype(BF16)
        for r in range(GRID_ROWS):
            rs = min(max(r - NA_ROWS // 2, 0), GRID_ROWS - NA_ROWS)
            off = rs - r + NA_ROWS - 1
            q = q_sc[h, r * GRID_W:(r + 1) * GRID_W, :]
            kw = k_sc[h, rs * GRID_W:(rs + NA_ROWS) * GRID_W, :]
            vw = v_sc[h, rs * GRID_W:(rs + NA_ROWS) * GRID_W, :]
            bias = jnp.concatenate([tb_ref[h, off + 2 * m] for m in range(NA_ROWS // 2)], axis=-1)
            s_loc = _dot_nt(q, kw) + bias
            s_ctx = _dot_nt(q, ckh)
            o_sc[h, r * GRID_W:(r + 1) * GRID_W, :] = _softmax_pv([(s_loc, vw), (s_ctx, cvh)])
        return carry

    lax.fori_loop(0, NA_HEADS, head_body, 0)
    for h in range(NA_HEADS):
        lo = D_FOURIER + h * NA_HEAD_DIM
        o_ref[:, lo:lo + NA_HEAD_DIM] = o_sc[h]


def _mix_sample(z, bin_prev, cache_k, cache_v, bias_tbl, layer, consts, wpg_bd, pool_scale):
    cl, sl, cb, sb = consts
    first = T_PROMPT // DEC_SEQ
    n_dr = 2 * NA_ROWS - 2
    cache_spec = pl.BlockSpec((1, 1, NA_HEADS, PAST_LEN, NA_HEAD_DIM), lambda b: (b, layer, 0, 0, 0))
    return pl.pallas_call(
        _mix_sample_kernel,
        grid=(DEC_BATCH,),
        in_specs=[
            pl.BlockSpec((DEC_SEQ, D_IN), lambda b: (first + b, 0)),
            pl.BlockSpec(memory_space=pl.ANY),
            cache_spec, cache_spec,
            pl.BlockSpec((NA_HEADS, n_dr, GRID_W, 128), lambda b: (layer, 0, 0, 0)),
            _const_spec(cl.shape), _const_spec(sl.shape), _const_spec(cb.shape), _const_spec(sb.shape),
            _const_spec(wpg_bd.shape), _const_spec(pool_scale.shape),
        ],
        out_specs=pl.BlockSpec((DEC_SEQ, D_MODEL), lambda b: (first + b, 0)),
        out_shape=jax.ShapeDtypeStruct((T_ALL, D_MODEL), F32),
        scratch_shapes=[
            pltpu.VMEM((NA_HEADS, DEC_SEQ, NA_HEAD_DIM), BF16),
            pltpu.VMEM((NA_HEADS, DEC_SEQ, NA_HEAD_DIM), BF16),
            pltpu.VMEM((NA_HEADS, DEC_SEQ, NA_HEAD_DIM), BF16),
            pltpu.VMEM((NA_HEADS, DEC_SEQ, NA_HEAD_DIM), F32),
        ],
        input_output_aliases={1: 0},
        compiler_params=_cparams(("arbitrary",)),
        name="mix_sample",
    )(z, bin_prev, cache_k, cache_v, bias_tbl, cl, sl, cb, sb, wpg_bd, pool_scale)


def _merge_kernel(x_ref, b_ref, mod_ref, ga_ref, gf_ref, wg_ref, bg_ref, wf_ref, wa_ref, wp_ref, wo_ref,
                  xo_ref, h2t_ref):
    x = x_ref[...]
    mod = mod_ref[0]
    h = _rms(x, ga_ref[...]) * (1.0 + mod[:, D_MODEL:2 * D_MODEL]) + mod[:, 0:D_MODEL]
    gates = jax.nn.sigmoid(_dot(h.astype(BF16), wg_ref[...]) + bg_ref[...])
    bf = _dot(b_ref[:, 0:D_FOURIER].astype(BF16), wf_ref[...])
    ba = _dot(b_ref[:, D_FOURIER:D_FOURIER + D_NA].astype(BF16), wa_ref[...])
    bp = _dot(b_ref[:, D_FOURIER + D_NA:D_MODEL].astype(BF16), wp_ref[...])
    merged = (gates[:, 0:D_MODEL] * bf + gates[:, D_MODEL:2 * D_MODEL] * ba
              + gates[:, 2 * D_MODEL:3 * D_MODEL] * bp)
    xn = x + mod[:, 2 * D_MODEL:3 * D_MODEL] * _dot(merged.astype(BF16), wo_ref[...])
    xo_ref[...] = xn
    h2 = _rms(xn, gf_ref[...]) * (1.0 + mod[:, 4 * D_MODEL:5 * D_MODEL]) + mod[:, 3 * D_MODEL:4 * D_MODEL]
    h2t_ref[...] = h2.T.astype(BF16)


def _merge(x, bins, mods, layer, g_attn, g_ffn, wg, bg, wf, wa, wp, wo):
    t = x.shape[0]
    row = pl.BlockSpec((TOKEN_TILE, D_MODEL), lambda i: (i, 0))
    return pl.pallas_call(
        _merge_kernel,
        grid=(t // TOKEN_TILE,),
        in_specs=[
            row, row,
            pl.BlockSpec((1, 1, N_MOD), lambda i: (layer * N_MOD_ROWS + _mod_row(i, TOKEN_TILE), 0, 0)),
            _const_spec(g_attn.shape), _const_spec(g_ffn.shape),
            _const_spec(wg.shape), _const_spec(bg.shape),
            _const_spec(wf.shape), _const_spec(wa.shape), _const_spec(wp.shape), _const_spec(wo.shape),
        ],
        out_specs=[row, pl.BlockSpec((D_MODEL, TOKEN_TILE), lambda i: (0, i))],
        out_shape=[jax.ShapeDtypeStruct((t, D_MODEL), F32), jax.ShapeDtypeStruct((D_MODEL, t), BF16)],
        compiler_params=_cparams(("arbitrary",)),
        name="merge",
    )(x, bins, mods, g_attn, g_ffn, wg, bg, wf, wa, wp, wo)


def _sort_pairs(n):
    pairs = []
    p = 1
    while p < n:
        k = p
        while k >= 1:
            for j in range(k % p, n - k, 2 * k):
                for i in range(min(k, n - j - k)):
                    if (i + j) // (2 * p) == (i + j + k) // (2 * p):
                        pairs.append((i + j, i + j + k))
            k //= 2
        p *= 2
    return pairs


def _sort_desc(xs):
    xs = list(xs)
    for i, j in _sort_pairs(len(xs)):
        a, b = xs[i], xs[j]
        xs[i], xs[j] = jnp.maximum(a, b), jnp.minimum(a, b)
    return xs


def _merge_top(xs, ys):
    n = len(xs)
    t = [jnp.maximum(xs[k], ys[n - 1 - k]) for k in range(n)]
    k = n // 2
    while k >= 1:
        for i in range(n):
            if i & k == 0:
                a, b = t[i], t[i + k]
                t[i], t[i + k] = jnp.maximum(a, b), jnp.minimum(a, b)
        k //= 2
    return t


def _peer_front_kernel(h2t_ref, wq_ref, kb1_ref, kb2_ref, kh2_ref, thr_ref, e1_ref, s2_ref, e2_ref,
                       s1_sc, s2_sc, row_sc):
    tt = h2t_ref.shape[1]
    nk = PEER_N_KEYS
    half = PEER_HEADS * nk
    h2t = h2t_ref[...]
    qp1 = _dot(wq_ref[0:half, :], h2t).astype(BF16)
    qp2 = _dot(wq_ref[half:2 * half, :], h2t).astype(BF16)
    s1_sc[...] = _dot(kb1_ref[...], qp1)
    s2_sc[...] = _dot(kb2_ref[...], qp2)
    s2_ref[...] = _dot(kh2_ref[...], qp2)

    def chunk(c, carry):
        col = pl.multiple_of(c * 128, 128)
        neg = jnp.full((PEER_HEADS, 128), NEG_BIG, F32)
        big = jnp.full((PEER_HEADS, 128), -NEG_BIG, F32)

        def top16(sc):
            groups = []
            for g in range(nk // PEER_TOPK):
                rows = [sc[(g * PEER_TOPK + m) * 8:(g * PEER_TOPK + m + 1) * 8, pl.ds(col, 128)]
                        for m in range(PEER_TOPK)]
                groups.append(_sort_desc(rows))
            while len(groups) > 1:
                groups = [_merge_top(groups[i], groups[i + 1]) for i in range(0, len(groups), 2)]
            return groups[0]

        a = top16(s1_sc)
        b = top16(s2_sc)
        cand = [[a[r] + b[k] for k in range(PEER_TOPK // (r + 1))] for r in range(PEER_TOPK)]
        best = _merge_top(cand[0], cand[1] + [neg] * (PEER_TOPK - len(cand[1])))
        rest = [v for row in cand[2:] for v in row]
        rest = _sort_desc(rest + [neg] * (32 - len(rest)))[:PEER_TOPK]
        tau = _merge_top(best, rest)[PEER_TOPK - 1]

        z = jnp.zeros((PEER_HEADS, 128), F32)
        thr_r = []
        for r in range(PEER_TOPK):
            n = len(cand[r])
            ind = [cand[r][k] >= tau for k in range(n)]
            last = big
            for k in range(n):
                last = jnp.where(ind[k], b[k], last)
                z = z + jnp.where(ind[k], jnp.exp(cand[r][k] - cand[0][0]), 0.0)
            if n < PEER_TOPK:
                nxt = b[n]
                for k in reversed(range(n)):
                    nxt = jnp.where(ind[k], nxt, b[k])
                mid = 0.5 * (last + nxt)
            else:
                nxt = b[n - 1]
                for k in reversed(range(n - 1)):
                    nxt = jnp.where(ind[k], nxt, b[k])
                mid = jnp.where(ind[n - 1], last, 0.5 * (last + nxt))
            thr_r.append(jnp.where(ind[0], mid, big))

        row_sc[0:8, :] = b[0]
        row_sc[8:16, :] = 1.0 / z
        for i in range(nk):
            s = s1_sc[i * 8:(i + 1) * 8, pl.ds(col, 128)]
            t = big
            for r in range(PEER_TOPK):
                t = jnp.where(s == a[r], thr_r[r], t)
            thr_ref[i * 8:(i + 1) * 8, pl.ds(col, 128)] = t
            e1_ref[i * 8:(i + 1) * 8, pl.ds(col, 128)] = jnp.exp(s - a[0])
        for h in range(PEER_HEADS):
            s = s2_ref[h * nk:(h + 1) * nk, pl.ds(col, 128)]
            e2_ref[h * nk:(h + 1) * nk, pl.ds(col, 128)] = (
                jnp.exp(s - row_sc[h:h + 1, :]) * row_sc[8 + h:9 + h, :])
        return carry

    lax.fori_loop(0, tt // 128, chunk, 0)


def _peer_front(h2t, wq_perm, kb1, kb2, kh2):
    t = h2t.shape[1]
    tt = PEER_TOKEN_TILE
    rows = PEER_HEADS * PEER_N_KEYS
    tbl = jax.ShapeDtypeStruct((rows, t), F32)
    tbl_spec = pl.BlockSpec((rows, tt), lambda i: (0, i))
    return pl.pallas_call(
        _peer_front_kernel,
        grid=(t // tt,),
        in_specs=[
            pl.BlockSpec((D_MODEL, tt), lambda i: (0, i)),
            _const_spec(wq_perm.shape), _const_spec(kb1.shape), _const_spec(kb2.shape), _const_spec(kh2.shape),
        ],
        out_specs=[tbl_spec] * 4,
        out_shape=[tbl] * 4,
        scratch_shapes=[
            pltpu.VMEM((rows, tt), F32),
            pltpu.VMEM((rows, tt), F32),
            pltpu.VMEM((16, 128), F32),
        ],
        compiler_params=_cparams(("arbitrary",)),
        name="peer_front",
    )(h2t, wq_perm, kb1, kb2, kh2)


def _peer_gate_block(at_ref, gt_ref, thr_ref, e1_ref, s2_ref, e2_ref):
    nk = PEER_N_KEYS
    sub = 16
    for c in range(at_ref.shape[1] // 128):
        lanes = slice(c * 128, (c + 1) * 128)
        for jb in range(nk // sub):
            w = [jnp.zeros((sub, 128), F32) for _ in range(PEER_I_BLOCK)]
            for h in range(PEER_HEADS):
                s2 = s2_ref[h * nk + jb * sub:h * nk + (jb + 1) * sub, lanes]
                e2 = e2_ref[h * nk + jb * sub:h * nk + (jb + 1) * sub, lanes]
                for ii in range(PEER_I_BLOCK):
                    row = ii * PEER_HEADS + h
                    hit = s2 >= thr_ref[row:row + 1, lanes]
                    w[ii] = w[ii] + jnp.where(hit, e2, 0.0) * e1_ref[row:row + 1, lanes]
            for ii in range(PEER_I_BLOCK):
                rows = slice(ii * nk + jb * sub, ii * nk + (jb + 1) * sub)
                gt_ref[rows, lanes] = (jax.nn.gelu(at_ref[rows, lanes]) * w[ii]).astype(BF16)


def _peer_dense_kernel(h2t_ref, u0_ref, u1_ref, v0_ref, v1_ref, thr0_ref, thr1_ref, e10_ref, e11_ref,
                       s2_ref, e2_ref, x_ref, mod_ref, o_ref, acc_sc, at_sc, gt_sc):
    local = pl.program_id(0) % PEER_STEPS_PER_TILE

    @pl.when(local == 0)
    def _():
        acc_sc[...] = jnp.zeros_like(acc_sc)

    h2t = h2t_ref[...]
    for p, (u_ref, v_ref, thr_ref, e1_ref) in enumerate(
            ((u0_ref, v0_ref, thr0_ref, e10_ref), (u1_ref, v1_ref, thr1_ref, e11_ref))):
        at_sc[p] = _dot(u_ref[...], h2t)
        _peer_gate_block(at_sc.at[p], gt_sc.at[p], thr_ref, e1_ref, s2_ref, e2_ref)
        acc_sc[...] += _dot(v_ref[...], gt_sc[p])

    @pl.when(local == PEER_STEPS_PER_TILE - 1)
    def _():
        o_ref[...] = x_ref[...] + mod_ref[0][:, 5 * D_MODEL:6 * D_MODEL] * acc_sc[...].T


def _peer_dense(h2t, u_bf, vt_bf, tables, x, mods, layer):
    thr, e1, s2, e2 = tables
    t = x.shape[0]
    tt = PEER_TOKEN_TILE
    eb = PEER_I_BLOCK * PEER_N_KEYS
    last = PEER_N_BLOCKS - 1

    def tile(g):
        return g // PEER_STEPS_PER_TILE

    def block(g, sub, lag):
        return jnp.clip((g % PEER_STEPS_PER_TILE) * 2 + sub - lag, 0, last)

    def u_spec(sub):
        return pl.BlockSpec((eb, D_MODEL), lambda g: (block(g, sub, 0), 0))

    def v_spec(sub):
        return pl.BlockSpec((D_MODEL, eb), lambda g: (0, block(g, sub, 0)))

    def i_spec(sub):
        return pl.BlockSpec((PEER_I_BLOCK * PEER_HEADS, tt), lambda g: (block(g, sub, 0), tile(g)))

    j_spec = pl.BlockSpec((PEER_HEADS * PEER_N_KEYS, tt), lambda g: (0, tile(g)))
    row = pl.BlockSpec((tt, D_MODEL), lambda g: (tile(g), 0))
    return pl.pallas_call(
        _peer_dense_kernel,
        grid=(t // tt * PEER_STEPS_PER_TILE,),
        in_specs=[
            pl.BlockSpec((D_MODEL, tt), lambda g: (0, tile(g))),
            u_spec(0), u_spec(1), v_spec(0), v_spec(1),
            i_spec(0), i_spec(1), i_spec(0), i_spec(1),
            j_spec, j_spec,
            row,
            pl.BlockSpec((1, 1, N_MOD), lambda g: (layer * N_MOD_ROWS + _mod_row(tile(g), tt), 0, 0)),
        ],
        out_specs=row,
        out_shape=jax.ShapeDtypeStruct((t, D_MODEL), F32),
        scratch_shapes=[
            pltpu.VMEM((D_MODEL, tt), F32),
            pltpu.VMEM((2, eb, tt), F32),
            pltpu.VMEM((2, eb, tt), BF16),
        ],
        compiler_params=_cparams(("arbitrary",)),
        name="peer_dense",
    )(h2t, u_bf, u_bf, vt_bf, vt_bf, thr, thr, e1, e1, s2, e2, x, mods)


def _final_norm_kernel(x_ref, g_ref, o_ref):
    o_ref[...] = _rms(x_ref[...], g_ref[...])


def _final_norm(x, g):
    t = x.shape[0]
    row = pl.BlockSpec((TOKEN_TILE, D_MODEL), lambda i: (i, 0))
    return pl.pallas_call(
        _final_norm_kernel,
        grid=(t // TOKEN_TILE,),
        in_specs=[row, _const_spec(g.shape)],
        out_specs=row,
        out_shape=jax.ShapeDtypeStruct((t, D_MODEL), F32),
        compiler_params=_cparams(("arbitrary",)),
        name="final_norm",
    )(x, g)


def _dft_consts(length):
    def cs(n):
        k = np.arange(n, dtype=np.float64)
        ang = 2.0 * np.pi * np.outer(k, k) / n
        return np.cos(ang) / np.sqrt(n), np.sin(ang) / np.sqrt(n)

    cl, sl = cs(length)
    cg, sg = cs(FOURIER_GROUP_DIM)
    eye = np.eye(D_FOURIER // FOURIER_GROUP_DIM)
    return tuple(jnp.asarray(m, F32) for m in (cl, sl, np.kron(eye, cg), np.kron(eye, sg)))


def _block_diag(w):
    g, c, e = w.shape
    eye = jnp.eye(g, dtype=w.dtype)
    return (eye[:, None, :, None] * w[:, :, None, :]).reshape(g * c, g * e)


def _peer_query_weights(wq, keys):
    nh, nk = PEER_HEADS, PEER_N_KEYS
    dk = keys.shape[-1]
    wq_perm = wq.reshape(D_MODEL, nh, 2, dk).transpose(2, 1, 3, 0).reshape(2 * nh * dk, D_MODEL)
    eye = jnp.eye(nh, dtype=keys.dtype)
    kb = [jnp.einsum('hkc,hg->khgc', keys[:, p], eye).reshape(nk * nh, nh * dk) for p in range(2)]
    kh2 = jnp.einsum('hkc,hg->hkgc', keys[:, 1], eye).reshape(nh * nk, nh * dk)
    return wq_perm.astype(BF16), kb[0].astype(BF16), kb[1].astype(BF16), kh2.astype(BF16)


def kernel(x_prompt, x_sample, cache_k, cache_v, c, c_ctx, w_ada, b_ada, g_attn, g_ffn, w_in, w_fourier,
           w_pool_grp, pool_scale, w_pool, rel_bias, w_attn, w_gate, b_gate, w_out, peer_wq, peer_keys,
           peer_u, peer_v, g_final):
    x = jnp.concatenate([x_prompt.reshape(T_PROMPT, D_MODEL), x_sample.reshape(T_SAMPLE, D_MODEL)], axis=0)
    cvec = jnp.zeros((N_MOD_ROWS, D_MODEL), F32).at[0].set(c_ctx).at[1:1 + DEC_BATCH].set(c)
    mods = _adaln(cvec, w_ada, b_ada).reshape(DEPTH * N_MOD_ROWS, 1, N_MOD)
    bias_tbl = _na_bias_table(rel_bias)
    consts_p = _dft_consts(SEQ)
    consts_s = _dft_consts(DEC_SEQ)

    new_k, new_v = [], []
    for l in range(DEPTH):
        ga = g_attn[l].reshape(1, D_MODEL)
        gf = g_ffn[l].reshape(1, D_MODEL)
        wpg_bd = _block_diag(w_pool_grp[l]).astype(BF16)
        ps = pool_scale[l].reshape(1, D_POOL)
        z = _front(x, mods, l, ga, w_in[l].astype(BF16))
        bins, kl, vl = _mix_prompt(z, consts_p, wpg_bd, ps)
        bins = _mix_sample(z, bins, cache_k, cache_v, bias_tbl, l, consts_s, wpg_bd, ps)
        new_k.append(kl)
        new_v.append(vl)
        x, h2t = _merge(x, bins, mods, l, ga, gf, w_gate[l].astype(BF16), b_gate[l].reshape(1, -1),
                        w_fourier[l].astype(BF16), w_attn[l].astype(BF16), w_pool[l].astype(BF16),
                        w_out[l].astype(BF16))
        tables = _peer_front(h2t, *_peer_query_weights(peer_wq[l], peer_keys[l]))
        x = _peer_dense(h2t, peer_u[l].astype(BF16), peer_v[l].T.astype(BF16), tables, x, mods, l)

    y = _final_norm(x, g_final.reshape(1, D_MODEL))
    y_prompt = y[:T_PROMPT].reshape(BATCH, SEQ, D_MODEL)
    y_sample = y[T_PROMPT:].reshape(DEC_BATCH, DEC_SEQ, D_MODEL)
    return y_prompt, y_sample, jnp.stack(new_k, axis=1), jnp.stack(new_v, axis=1)
```

```python
import functools

import numpy as np
import jax
import jax.numpy as jnp
from jax import lax
from jax.experimental import pallas as pl
from jax.experimental.pallas import tpu as pltpu

F32 = jnp.float32
BF16 = jnp.bfloat16

D_MODEL = 1024
BATCH = 16
SEQ = 256
DEPTH = 2
DEC_BATCH = 2
DEC_SEQ = 1024
PAST_LEN = 256
GRID_W = 64
GRID_ROWS = DEC_SEQ // GRID_W
D_FOURIER = 256
FOURIER_GROUP_DIM = 64
NA_HEADS = 8
NA_HEAD_DIM = 64
D_NA = NA_HEADS * NA_HEAD_DIM
NA_ROWS = 8
NA_COLS = 16
POOL_HALVES = (1, 2, 4, 8)
D_POOL = 256
POOL_GROUP_DIM = 64
D_IN = D_FOURIER + 3 * D_NA + D_POOL
Q_OFF = D_FOURIER
K_OFF = D_FOURIER + D_NA
V_OFF = D_FOURIER + 2 * D_NA
P_OFF = D_FOURIER + 3 * D_NA
PEER_HEADS = 8
PEER_N_KEYS = 128
PEER_N_EXPERTS = PEER_N_KEYS * PEER_N_KEYS
PEER_TOPK = 16
EPS = 1e-6
NEG_INF = -1e30
NEG_BIG = -3.0e38

T_PROMPT = BATCH * SEQ
T_SAMPLE = DEC_BATCH * DEC_SEQ
T_ALL = T_PROMPT + T_SAMPLE
N_MOD_ROWS = 8
N_MOD = 6 * D_MODEL

TOKEN_TILE = 256
PEER_TOKEN_TILE = 512
PEER_I_BLOCK = 8
PEER_N_BLOCKS = PEER_N_KEYS // PEER_I_BLOCK
PEER_STEPS_PER_TILE = PEER_N_BLOCKS // 2
VMEM_LIMIT = 56 * 1024 * 1024


def _cparams(sem):
    return pltpu.CompilerParams(dimension_semantics=sem, vmem_limit_bytes=VMEM_LIMIT)


def _mod_row(i, tile):
    n_prompt = T_PROMPT // tile
    per_batch = DEC_SEQ // tile
    return jnp.where(i < n_prompt, 0, 1 + (i - n_prompt) // per_batch)


def _rms(x, g):
    return x * lax.rsqrt(jnp.mean(x * x, axis=-1, keepdims=True) + EPS) * g


def _split_bf16(x):
    hi = x.astype(BF16)
    lo = (x - hi.astype(F32)).astype(BF16)
    return hi, lo


def _dot(a, b):
    return jnp.dot(a, b, preferred_element_type=F32)


def _dot_nt(a, b):
    return lax.dot_general(a, b, (((1,), (1,)), ((), ())), preferred_element_type=F32)


def _adaln_kernel(c_ref, w_ref, b_ref, o_ref):
    c = c_ref[...]
    s = c * jax.nn.sigmoid(c)
    s_hi, s_lo = _split_bf16(s)
    w_hi, w_lo = _split_bf16(w_ref[0])
    acc = _dot(s_hi, w_hi) + _dot(s_hi, w_lo) + _dot(s_lo, w_hi)
    o_ref[0] = acc + b_ref[0]


def _adaln(cvec, w_ada, b_ada):
    nb = 1536
    return pl.pallas_call(
        _adaln_kernel,
        grid=(DEPTH, N_MOD // nb),
        in_specs=[
            pl.BlockSpec((N_MOD_ROWS, D_MODEL), lambda l, j: (0, 0)),
            pl.BlockSpec((1, D_MODEL, nb), lambda l, j: (l, 0, j)),
            pl.BlockSpec((1, 1, nb), lambda l, j: (l, 0, j)),
        ],
        out_specs=pl.BlockSpec((1, N_MOD_ROWS, nb), lambda l, j: (l, 0, j)),
        out_shape=jax.ShapeDtypeStruct((DEPTH, N_MOD_ROWS, N_MOD), F32),
        compiler_params=_cparams(("arbitrary", "arbitrary")),
        name="adaln",
    )(cvec, w_ada, b_ada.reshape(DEPTH, 1, N_MOD))


def _na_bias_kernel(rb_ref, o_ref):
    g = pl.program_id(0)
    n_dr = 2 * NA_ROWS - 1
    n_dc = 2 * NA_COLS - 1
    qc = lax.broadcasted_iota(jnp.int32, (GRID_W, 128), 0)
    lane = lax.broadcasted_iota(jnp.int32, (GRID_W, 128), 1)
    kc = lane & (GRID_W - 1)
    upper = lane >= GRID_W
    dcc = jnp.clip(kc - qc, -(NA_COLS - 1), NA_COLS - 1) + (NA_COLS - 1)
    cstart = jnp.clip(qc - NA_COLS // 2, 0, GRID_W - NA_COLS)
    valid = (kc >= cstart) & (kc < cstart + NA_COLS)
    base = g * (n_dr * n_dc)
    planes = []
    for dr in range(n_dr):
        acc = jnp.zeros((GRID_W, 128), F32)
        for d in range(n_dc):
            acc = jnp.where(dcc == d, rb_ref[base + dr * n_dc + d], acc)
        planes.append(acc)
    for dr in range(n_dr - 1):
        o_ref[0, dr] = jnp.where(valid, jnp.where(upper, planes[dr + 1], planes[dr]), NEG_INF)


def _na_bias_table(rel_bias):
    n = DEPTH * NA_HEADS
    return pl.pallas_call(
        _na_bias_kernel,
        grid=(n,),
        in_specs=[pl.BlockSpec(memory_space=pltpu.SMEM)],
        out_specs=pl.BlockSpec((1, 2 * NA_ROWS - 2, GRID_W, 128), lambda g: (g, 0, 0, 0)),
        out_shape=jax.ShapeDtypeStruct((n, 2 * NA_ROWS - 2, GRID_W, 128), F32),
        compiler_params=_cparams(("arbitrary",)),
        name="na_bias",
    )(rel_bias.reshape(-1))


def _front_kernel(x_ref, mod_ref, g_ref, w_ref, z_ref):
    mod = mod_ref[0]
    h = _rms(x_ref[...], g_ref[...]) * (1.0 + mod[:, D_MODEL:2 * D_MODEL]) + mod[:, 0:D_MODEL]
    z_ref[...] = _dot(h.astype(BF16), w_ref[...])


def _front(x, mods, layer, g_attn, w_in_bf):
    t = x.shape[0]
    return pl.pallas_call(
        _front_kernel,
        grid=(t // TOKEN_TILE,),
        in_specs=[
            pl.BlockSpec((TOKEN_TILE, D_MODEL), lambda i: (i, 0)),
            pl.BlockSpec((1, 1, N_MOD), lambda i: (layer * N_MOD_ROWS + _mod_row(i, TOKEN_TILE), 0, 0)),
            pl.BlockSpec((1, D_MODEL), lambda i: (0, 0)),
            pl.BlockSpec((D_MODEL, D_IN), lambda i: (0, 0)),
        ],
        out_specs=pl.BlockSpec((TOKEN_TILE, D_IN), lambda i: (i, 0)),
        out_shape=jax.ShapeDtypeStruct((t, D_IN), F32),
        compiler_params=_cparams(("arbitrary",)),
        name="front",
    )(x, mods, g_attn, w_in_bf)


def _fourier(zf, cl_ref, sl_ref, cb_ref, sb_ref):
    zb = zf.astype(BF16)
    zc = _dot(zb, cb_ref[...].astype(BF16))
    zs = _dot(zb, sb_ref[...].astype(BF16))
    return (_dot(cl_ref[...].astype(BF16), zc.astype(BF16))
            - _dot(sl_ref[...].astype(BF16), zs.astype(BF16)))


def _pool(zp, wpg_ref, ps_ref):
    length = zp.shape[0]
    t = lax.broadcasted_iota(jnp.int32, zp.shape, 0)
    group = lax.broadcasted_iota(jnp.int32, zp.shape, 1) // POOL_GROUP_DIM

    def down(x, k):
        return jnp.where(t >= k, pltpu.roll(x, k, 0), 0.0)

    def up(x, k):
        return jnp.where(t < length - k, pltpu.roll(x, length - k, 0), 0.0)

    fwd = [zp]
    bwd = [down(zp, 1)]
    for n in range(3):
        fwd.append(fwd[n] + up(fwd[n], 1 << n))
        bwd.append(bwd[n] + down(bwd[n], 1 << n))
    win = fwd[3] + bwd[3]
    half = jnp.full(zp.shape, POOL_HALVES[3], jnp.int32)
    for n in (2, 1, 0):
        win = jnp.where(group == n, fwd[n] + bwd[n], win)
        half = jnp.where(group == n, POOL_HALVES[n], half)
    cnt = jnp.minimum(t + half, length) - jnp.maximum(t - half, 0)
    d = win / cnt.astype(F32) - zp
    return _dot(d.astype(BF16), wpg_ref[...]) * ps_ref[...]


def _softmax_pv(parts):
    m = parts[0][0].max(axis=-1, keepdims=True)
    for s, _ in parts[1:]:
        m = jnp.maximum(m, s.max(axis=-1, keepdims=True))
    den = 0.0
    acc = 0.0
    for s, v in parts:
        p = jnp.exp(s - m)
        den = den + p.sum(axis=-1, keepdims=True)
        acc = acc + _dot(p.astype(BF16), v)
    return acc / den


def _mix_prompt_kernel(z_ref, cl_ref, sl_ref, cb_ref, sb_ref, wpg_ref, ps_ref, *refs, n_prev):
    if n_prev:
        kp_ref, vp_ref, o_ref, ko_ref, vo_ref = refs
        ko_ref[0, 0:n_prev] = kp_ref[0]
        vo_ref[0, 0:n_prev] = vp_ref[0]
    else:
        o_ref, ko_ref, vo_ref = refs
    o_ref[:, 0:D_FOURIER] = _fourier(z_ref[:, 0:D_FOURIER], cl_ref, sl_ref, cb_ref, sb_ref)
    o_ref[:, D_FOURIER + D_NA:D_MODEL] = _pool(z_ref[:, P_OFF:D_IN], wpg_ref, ps_ref)
    for h in range(NA_HEADS):
        lo = h * NA_HEAD_DIM
        q = z_ref[:, Q_OFF + lo:Q_OFF + lo + NA_HEAD_DIM] * (NA_HEAD_DIM ** -0.5)
        k = z_ref[:, K_OFF + lo:K_OFF + lo + NA_HEAD_DIM]
        v = z_ref[:, V_OFF + lo:V_OFF + lo + NA_HEAD_DIM]
        ko_ref[0, n_prev, h] = k
        vo_ref[0, n_prev, h] = v
        s = _dot_nt(q.astype(BF16), k.astype(BF16))
        o_ref[:, D_FOURIER + lo:D_FOURIER + lo + NA_HEAD_DIM] = _softmax_pv([(s, v.astype(BF16))])


def _const_spec(shape):
    nd = len(shape)
    return pl.BlockSpec(shape, lambda *_: (0,) * nd)


def _mix_prompt(z, consts, wpg_bd, pool_scale, prev_kv):
    cl, sl, cb, sb = consts
    n_prev = 0 if prev_kv is None else prev_kv[0].shape[1]

    def kv_spec(n):
        return pl.BlockSpec((1, n, NA_HEADS, SEQ, NA_HEAD_DIM), lambda b: (b, 0, 0, 0, 0))

    kv_shape = jax.ShapeDtypeStruct((BATCH, n_prev + 1, NA_HEADS, SEQ, NA_HEAD_DIM), F32)
    return pl.pallas_call(
        functools.partial(_mix_prompt_kernel, n_prev=n_prev),
        grid=(BATCH,),
        in_specs=[
            pl.BlockSpec((SEQ, D_IN), lambda b: (b, 0)),
            _const_spec(cl.shape), _const_spec(sl.shape), _const_spec(cb.shape), _const_spec(sb.shape),
            _const_spec(wpg_bd.shape), _const_spec(pool_scale.shape),
        ] + ([kv_spec(n_prev)] * 2 if n_prev else []),
        out_specs=[pl.BlockSpec((SEQ, D_MODEL), lambda b: (b, 0)), kv_spec(n_prev + 1), kv_spec(n_prev + 1)],
        out_shape=[jax.ShapeDtypeStruct((T_PROMPT, D_MODEL), F32), kv_shape, kv_shape],
        compiler_params=_cparams(("arbitrary",)),
        name="mix_prompt",
    )(z, cl, sl, cb, sb, wpg_bd, pool_scale, *(prev_kv or ()))


def _mix_sample_kernel(z_ref, ck_ref, cv_ref, tb_ref, cl_ref, sl_ref, cb_ref, sb_ref, wpg_ref, ps_ref,
                       o_ref, q_sc, k_sc, v_sc, o_sc):
    o_ref[:, 0:D_FOURIER] = _fourier(z_ref[:, 0:D_FOURIER], cl_ref, sl_ref, cb_ref, sb_ref)
    o_ref[:, D_FOURIER + D_NA:D_MODEL] = _pool(z_ref[:, P_OFF:D_IN], wpg_ref, ps_ref)
    for h in range(NA_HEADS):
        lo = h * NA_HEAD_DIM
        q_sc[h] = (z_ref[:, Q_OFF + lo:Q_OFF + lo + NA_HEAD_DIM] * (NA_HEAD_DIM ** -0.5)).astype(BF16)
        k_sc[h] = z_ref[:, K_OFF + lo:K_OFF + lo + NA_HEAD_DIM].astype(BF16)
        v_sc[h] = z_ref[:, V_OFF + lo:V_OFF + lo + NA_HEAD_DIM].astype(BF16)

    def head_body(h, carry):
        ckh = ck_ref[0, 0, h].astype(BF16)
        cvh = cv_ref[0, 0, h].astype(BF16)
        for r in range(GRID_ROWS):
            rs = min(max(r - NA_ROWS // 2, 0), GRID_ROWS - NA_ROWS)
            off = rs - r + NA_ROWS - 1
            q = q_sc[h, r * GRID_W:(r + 1) * GRID_W, :]
            kw = k_sc[h, rs * GRID_W:(rs + NA_ROWS) * GRID_W, :]
            vw = v_sc[h, rs * GRID_W:(rs + NA_ROWS) * GRID_W, :]
            bias = jnp.concatenate([tb_ref[h, off + 2 * m] for m in range(NA_ROWS // 2)], axis=-1)
            s_loc = _dot_nt(q, kw) + bias
            s_ctx = _dot_nt(q, ckh)
            o_sc[h, r * GRID_W:(r + 1) * GRID_W, :] = _softmax_pv([(s_loc, vw), (s_ctx, cvh)])
        return carry

    lax.fori_loop(0, NA_HEADS, head_body, 0)
    for h in range(NA_HEADS):
        lo = D_FOURIER + h * NA_HEAD_DIM
        o_ref[:, lo:lo + NA_HEAD_DIM] = o_sc[h]


def _mix_sample(z, cache_k, cache_v, bias_tbl, layer, consts, wpg_bd, pool_scale):
    cl, sl, cb, sb = consts
    first = T_PROMPT // DEC_SEQ
    n_dr = 2 * NA_ROWS - 2
    cache_spec = pl.BlockSpec((1, 1, NA_HEADS, PAST_LEN, NA_HEAD_DIM), lambda b: (b, layer, 0, 0, 0))
    return pl.pallas_call(
        _mix_sample_kernel,
        grid=(DEC_BATCH,),
        in_specs=[
            pl.BlockSpec((DEC_SEQ, D_IN), lambda b: (first + b, 0)),
            cache_spec, cache_spec,
            pl.BlockSpec((NA_HEADS, n_dr, GRID_W, 128), lambda b: (layer, 0, 0, 0)),
            _const_spec(cl.shape), _const_spec(sl.shape), _const_spec(cb.shape), _const_spec(sb.shape),
            _const_spec(wpg_bd.shape), _const_spec(pool_scale.shape),
        ],
        out_specs=pl.BlockSpec((DEC_SEQ, D_MODEL), lambda b: (b, 0)),
        out_shape=jax.ShapeDtypeStruct((T_SAMPLE, D_MODEL), F32),
        scratch_shapes=[
            pltpu.VMEM((NA_HEADS, DEC_SEQ, NA_HEAD_DIM), BF16),
            pltpu.VMEM((NA_HEADS, DEC_SEQ, NA_HEAD_DIM), BF16),
            pltpu.VMEM((NA_HEADS, DEC_SEQ, NA_HEAD_DIM), BF16),
            pltpu.VMEM((NA_HEADS, DEC_SEQ, NA_HEAD_DIM), F32),
        ],
        compiler_params=_cparams(("arbitrary",)),
        name="mix_sample",
    )(z, cache_k, cache_v, bias_tbl, cl, sl, cb, sb, wpg_bd, pool_scale)


def _merge_kernel(x_ref, bp_ref, bs_ref, mod_ref, ga_ref, gf_ref, wg_ref, bg_ref, wf_ref, wa_ref, wp_ref, wo_ref,
                  xo_ref, h2t_ref):
    x = x_ref[...]
    mod = mod_ref[0]
    h = _rms(x, ga_ref[...]) * (1.0 + mod[:, D_MODEL:2 * D_MODEL]) + mod[:, 0:D_MODEL]
    gates = jax.nn.sigmoid(_dot(h.astype(BF16), wg_ref[...]) + bg_ref[...])
    is_prompt = pl.program_id(0) < T_PROMPT // TOKEN_TILE

    def branch(lo, hi):
        return jnp.where(is_prompt, bp_ref[:, lo:hi], bs_ref[:, lo:hi]).astype(BF16)

    bf = _dot(branch(0, D_FOURIER), wf_ref[...])
    ba = _dot(branch(D_FOURIER, D_FOURIER + D_NA), wa_ref[...])
    bp = _dot(branch(D_FOURIER + D_NA, D_MODEL), wp_ref[...])
    merged = (gates[:, 0:D_MODEL] * bf + gates[:, D_MODEL:2 * D_MODEL] * ba
              + gates[:, 2 * D_MODEL:3 * D_MODEL] * bp)
    xn = x + mod[:, 2 * D_MODEL:3 * D_MODEL] * _dot(merged.astype(BF16), wo_ref[...])
    xo_ref[...] = xn
    h2 = _rms(xn, gf_ref[...]) * (1.0 + mod[:, 4 * D_MODEL:5 * D_MODEL]) + mod[:, 3 * D_MODEL:4 * D_MODEL]
    h2t_ref[...] = h2.T.astype(BF16)


def _merge(x, bins_p, bins_s, mods, layer, g_attn, g_ffn, wg, bg, wf, wa, wp, wo):
    t = x.shape[0]
    n_p = T_PROMPT // TOKEN_TILE
    row = pl.BlockSpec((TOKEN_TILE, D_MODEL), lambda i: (i, 0))
    return pl.pallas_call(
        _merge_kernel,
        grid=(t // TOKEN_TILE,),
        in_specs=[
            row,
            pl.BlockSpec((TOKEN_TILE, D_MODEL), lambda i: (jnp.minimum(i, n_p - 1), 0)),
            pl.BlockSpec((TOKEN_TILE, D_MODEL), lambda i: (jnp.maximum(i - n_p, 0), 0)),
            pl.BlockSpec((1, 1, N_MOD), lambda i: (layer * N_MOD_ROWS + _mod_row(i, TOKEN_TILE), 0, 0)),
            _const_spec(g_attn.shape), _const_spec(g_ffn.shape),
            _const_spec(wg.shape), _const_spec(bg.shape),
            _const_spec(wf.shape), _const_spec(wa.shape), _const_spec(wp.shape), _const_spec(wo.shape),
        ],
        out_specs=[row, pl.BlockSpec((D_MODEL, TOKEN_TILE), lambda i: (0, i))],
        out_shape=[jax.ShapeDtypeStruct((t, D_MODEL), F32), jax.ShapeDtypeStruct((D_MODEL, t), BF16)],
        compiler_params=_cparams(("arbitrary",)),
        name="merge",
    )(x, bins_p, bins_s, mods, g_attn, g_ffn, wg, bg, wf, wa, wp, wo)


def _sort_pairs(n):
    pairs = []
    p = 1
    while p < n:
        k = p
        while k >= 1:
            for j in range(k % p, n - k, 2 * k):
                for i in range(min(k, n - j - k)):
                    if (i + j) // (2 * p) == (i + j + k) // (2 * p):
                        pairs.append((i + j, i + j + k))
            k //= 2
        p *= 2
    return pairs


def _sort_desc(xs):
    xs = list(xs)
    for i, j in _sort_pairs(len(xs)):
        a, b = xs[i], xs[j]
        xs[i], xs[j] = jnp.maximum(a, b), jnp.minimum(a, b)
    return xs


def _merge_top(xs, ys):
    n = len(xs)
    t = [jnp.maximum(xs[k], ys[n - 1 - k]) for k in range(n)]
    k = n // 2
    while k >= 1:
        for i in range(n):
            if i & k == 0:
                a, b = t[i], t[i + k]
                t[i], t[i + k] = jnp.maximum(a, b), jnp.minimum(a, b)
        k //= 2
    return t


def _peer_front_kernel(h2t_ref, wq_ref, kb1_ref, kb2_ref, kh2_ref, thr_ref, e1_ref, s2_ref, e2_ref,
                       s1_sc, s2_sc, row_sc):
    tt = h2t_ref.shape[1]
    nk = PEER_N_KEYS
    half = PEER_HEADS * nk
    h2t = h2t_ref[...]
    qp1 = _dot(wq_ref[0:half, :], h2t).astype(BF16)
    qp2 = _dot(wq_ref[half:2 * half, :], h2t).astype(BF16)
    s1_sc[...] = _dot(kb1_ref[...], qp1)
    s2_sc[...] = _dot(kb2_ref[...], qp2)
    s2_ref[...] = _dot(kh2_ref[...], qp2)

    def chunk(c, carry):
        col = pl.multiple_of(c * 128, 128)
        neg = jnp.full((PEER_HEADS, 128), NEG_BIG, F32)
        big = jnp.full((PEER_HEADS, 128), -NEG_BIG, F32)

        def top16(sc):
            groups = []
            for g in range(nk // PEER_TOPK):
                rows = [sc[(g * PEER_TOPK + m) * 8:(g * PEER_TOPK + m + 1) * 8, pl.ds(col, 128)]
                        for m in range(PEER_TOPK)]
                groups.append(_sort_desc(rows))
            while len(groups) > 1:
                groups = [_merge_top(groups[i], groups[i + 1]) for i in range(0, len(groups), 2)]
            return groups[0]

        a = top16(s1_sc)
        b = top16(s2_sc)
        cand = [[a[r] + b[k] for k in range(PEER_TOPK // (r + 1))] for r in range(PEER_TOPK)]
        best = _merge_top(cand[0], cand[1] + [neg] * (PEER_TOPK - len(cand[1])))
        rest = [v for row in cand[2:] for v in row]
        rest = _sort_desc(rest + [neg] * (32 - len(rest)))[:PEER_TOPK]
        tau = _merge_top(best, rest)[PEER_TOPK - 1]

        z = jnp.zeros((PEER_HEADS, 128), F32)
        thr_r = []
        for r in range(PEER_TOPK):
            n = len(cand[r])
            ind = [cand[r][k] >= tau for k in range(n)]
            last = big
            for k in range(n):
                last = jnp.where(ind[k], b[k], last)
                z = z + jnp.where(ind[k], jnp.exp(cand[r][k] - cand[0][0]), 0.0)
            if n < PEER_TOPK:
                nxt = b[n]
                for k in reversed(range(n)):
                    nxt = jnp.where(ind[k], nxt, b[k])
                mid = 0.5 * (last + nxt)
            else:
                nxt = b[n - 1]
                for k in reversed(range(n - 1)):
                    nxt = jnp.where(ind[k], nxt, b[k])
                mid = jnp.where(ind[n - 1], last, 0.5 * (last + nxt))
            thr_r.append(jnp.where(ind[0], mid, big))

        row_sc[0:8, :] = b[0]
        row_sc[8:16, :] = 1.0 / z
        for i in range(nk):
            s = s1_sc[i * 8:(i + 1) * 8, pl.ds(col, 128)]
            t = big
            for r in range(PEER_TOPK):
                t = jnp.where(s == a[r], thr_r[r], t)
            thr_ref[i * 8:(i + 1) * 8, pl.ds(col, 128)] = t
            e1_ref[i * 8:(i + 1) * 8, pl.ds(col, 128)] = 0.5 * jnp.exp(s - a[0])
        for h in range(PEER_HEADS):
            s = s2_ref[h * nk:(h + 1) * nk, pl.ds(col, 128)]
            e2_ref[h * nk:(h + 1) * nk, pl.ds(col, 128)] = (
                jnp.exp(s - row_sc[h:h + 1, :]) * row_sc[8 + h:9 + h, :])
        return carry

    lax.fori_loop(0, tt // 128, chunk, 0)


def _peer_front(h2t, wq_perm, kb1, kb2, kh2):
    t = h2t.shape[1]
    tt = PEER_TOKEN_TILE
    rows = PEER_HEADS * PEER_N_KEYS
    tbl = jax.ShapeDtypeStruct((rows, t), F32)
    tbl_spec = pl.BlockSpec((rows, tt), lambda i: (0, i))
    return pl.pallas_call(
        _peer_front_kernel,
        grid=(t // tt,),
        in_specs=[
            pl.BlockSpec((D_MODEL, tt), lambda i: (0, i)),
            _const_spec(wq_perm.shape), _const_spec(kb1.shape), _const_spec(kb2.shape), _const_spec(kh2.shape),
        ],
        out_specs=[tbl_spec] * 4,
        out_shape=[tbl] * 4,
        scratch_shapes=[
            pltpu.VMEM((rows, tt), F32),
            pltpu.VMEM((rows, tt), F32),
            pltpu.VMEM((16, 128), F32),
        ],
        compiler_params=_cparams(("arbitrary",)),
        name="peer_front",
    )(h2t, wq_perm, kb1, kb2, kh2)


GELU_K0 = float(np.sqrt(2.0 / np.pi))
GELU_K1 = 0.044715 * GELU_K0
PEER_GATE_ROWS = 16
PEER_MM_CHUNKS = 4


def _peer_gate_groups(at_ref, gt_ref, thr_ref, e1_ref, s2_ref, e2_ref):
    nk = PEER_N_KEYS
    sub = PEER_GATE_ROWS
    for c in range(at_ref.shape[1] // 128):
        lanes = slice(c * 128, (c + 1) * 128)
        for jb in range(nk // sub):
            w = [jnp.zeros((sub, 128), F32) for _ in range(PEER_I_BLOCK)]
            for h in range(PEER_HEADS):
                s2 = s2_ref[h * nk + jb * sub:h * nk + (jb + 1) * sub, lanes]
                e2 = e2_ref[h * nk + jb * sub:h * nk + (jb + 1) * sub, lanes]
                for ii in range(PEER_I_BLOCK):
                    row = ii * PEER_HEADS + h
                    hit = s2 >= thr_ref[row:row + 1, lanes]
                    w[ii] = w[ii] + jnp.where(hit, e2, 0.0) * e1_ref[row:row + 1, lanes]
            for ii in range(PEER_I_BLOCK):
                rows = slice(ii * nk + jb * sub, ii * nk + (jb + 1) * sub)
                a = at_ref[rows, lanes]
                th = jnp.tanh(a * (a * a * GELU_K1 + GELU_K0))
                gt_ref[rows, lanes] = ((a * th + a) * w[ii]).astype(BF16)
            yield


def _peer_dense_kernel(h2t_ref, u0_ref, u1_ref, v0_ref, v1_ref, thr0_ref, thr1_ref, e10_ref, e11_ref,
                       s2_ref, e2_ref, x_ref, mod_ref, gfin_ref, o_ref, acc_sc, at_sc, gt_sc, *, final):
    local = pl.program_id(0) % PEER_STEPS_PER_TILE

    @pl.when(local == 0)
    def _():
        acc_sc[...] = jnp.zeros_like(acc_sc)

    h2t = h2t_ref[...]
    u_refs, v_refs = (u0_ref, u1_ref), (v0_ref, v1_ref)
    eb = u0_ref.shape[0]
    ur = eb // PEER_MM_CHUNKS
    vr = D_MODEL // PEER_MM_CHUNKS

    def scores(b, k):
        at_sc[b, k * ur:(k + 1) * ur, :] = _dot(u_refs[b][k * ur:(k + 1) * ur, :], h2t)

    def accumulate(b, k):
        acc_sc[k * vr:(k + 1) * vr, :] += _dot(v_refs[b][k * vr:(k + 1) * vr, :], gt_sc[b])

    gates = [_peer_gate_groups(at_sc.at[b], gt_sc.at[b], thr_ref, e1_ref, s2_ref, e2_ref)
             for b, (thr_ref, e1_ref) in enumerate(((thr0_ref, e10_ref), (thr1_ref, e11_ref)))]
    groups_per_chunk = (h2t_ref.shape[1] // 128) * (PEER_N_KEYS // PEER_GATE_ROWS) // PEER_MM_CHUNKS
    for k in range(PEER_MM_CHUNKS):
        scores(0, k)
    for k in range(PEER_MM_CHUNKS):
        scores(1, k)
        for _ in range(groups_per_chunk):
            next(gates[0])
    for k in range(PEER_MM_CHUNKS):
        accumulate(0, k)
        for _ in range(groups_per_chunk):
            next(gates[1])
    for k in range(PEER_MM_CHUNKS):
        accumulate(1, k)

    @pl.when(local == PEER_STEPS_PER_TILE - 1)
    def _():
        xo = x_ref[...] + mod_ref[0][:, 5 * D_MODEL:6 * D_MODEL] * acc_sc[...].T
        o_ref[...] = _rms(xo, gfin_ref[...]) if final else xo


def _peer_dense(h2t, u_bf, vt_bf, tables, x, mods, layer, g_final, final):
    thr, e1, s2, e2 = tables
    t = x.shape[0]
    tt = PEER_TOKEN_TILE
    eb = PEER_I_BLOCK * PEER_N_KEYS
    last = PEER_N_BLOCKS - 1

    def tile(g):
        return g // PEER_STEPS_PER_TILE

    def block(g, sub, lag):
        return jnp.clip((g % PEER_STEPS_PER_TILE) * 2 + sub - lag, 0, last)

    def u_spec(sub):
        return pl.BlockSpec((None, eb, D_MODEL), lambda g: (layer, block(g, sub, 0), 0))

    def v_spec(sub):
        return pl.BlockSpec((None, D_MODEL, eb), lambda g: (layer, 0, block(g, sub, 0)))

    def i_spec(sub):
        return pl.BlockSpec((PEER_I_BLOCK * PEER_HEADS, tt), lambda g: (block(g, sub, 0), tile(g)))

    j_spec = pl.BlockSpec((PEER_HEADS * PEER_N_KEYS, tt), lambda g: (0, tile(g)))
    row = pl.BlockSpec((tt, D_MODEL), lambda g: (tile(g), 0))
    return pl.pallas_call(
        functools.partial(_peer_dense_kernel, final=final),
        grid=(t // tt * PEER_STEPS_PER_TILE,),
        in_specs=[
            pl.BlockSpec((D_MODEL, tt), lambda g: (0, tile(g))),
            u_spec(0), u_spec(1), v_spec(0), v_spec(1),
            i_spec(0), i_spec(1), i_spec(0), i_spec(1),
            j_spec, j_spec,
            row,
            pl.BlockSpec((1, 1, N_MOD), lambda g: (layer * N_MOD_ROWS + _mod_row(tile(g), tt), 0, 0)),
            _const_spec(g_final.shape),
        ],
        out_specs=row,
        out_shape=jax.ShapeDtypeStruct((t, D_MODEL), F32),
        scratch_shapes=[
            pltpu.VMEM((D_MODEL, tt), F32),
            pltpu.VMEM((2, eb, tt), F32),
            pltpu.VMEM((2, eb, tt), BF16),
        ],
        compiler_params=_cparams(("arbitrary",)),
        name="peer_dense",
    )(h2t, u_bf, u_bf, vt_bf, vt_bf, thr, thr, e1, e1, s2, e2, x, mods, g_final)


def _dft_consts(length):
    def cs(n):
        k = np.arange(n, dtype=np.float64)
        ang = 2.0 * np.pi * np.outer(k, k) / n
        return np.cos(ang) / np.sqrt(n), np.sin(ang) / np.sqrt(n)

    cl, sl = cs(length)
    cg, sg = cs(FOURIER_GROUP_DIM)
    eye = np.eye(D_FOURIER // FOURIER_GROUP_DIM)
    return tuple(jnp.asarray(m, F32) for m in (cl, sl, np.kron(eye, cg), np.kron(eye, sg)))


def _block_diag(w):
    g, c, e = w.shape
    eye = jnp.eye(g, dtype=w.dtype)
    return (eye[:, None, :, None] * w[:, :, None, :]).reshape(g * c, g * e)


def _peer_query_weights(wq, keys):
    nh, nk = PEER_HEADS, PEER_N_KEYS
    dk = keys.shape[-1]
    wq_perm = wq.reshape(D_MODEL, nh, 2, dk).transpose(2, 1, 3, 0).reshape(2 * nh * dk, D_MODEL)
    eye = jnp.eye(nh, dtype=keys.dtype)
    kb = [jnp.einsum('hkc,hg->khgc', keys[:, p], eye).reshape(nk * nh, nh * dk) for p in range(2)]
    kh2 = jnp.einsum('hkc,hg->hkgc', keys[:, 1], eye).reshape(nh * nk, nh * dk)
    return wq_perm.astype(BF16), kb[0].astype(BF16), kb[1].astype(BF16), kh2.astype(BF16)


def kernel(x_prompt, x_sample, cache_k, cache_v, c, c_ctx, w_ada, b_ada, g_attn, g_ffn, w_in, w_fourier,
           w_pool_grp, pool_scale, w_pool, rel_bias, w_attn, w_gate, b_gate, w_out, peer_wq, peer_keys,
           peer_u, peer_v, g_final):
    x = jnp.concatenate([x_prompt.reshape(T_PROMPT, D_MODEL), x_sample.reshape(T_SAMPLE, D_MODEL)], axis=0)
    cvec = jnp.zeros((N_MOD_ROWS, D_MODEL), F32).at[0].set(c_ctx).at[1:1 + DEC_BATCH].set(c)
    mods = _adaln(cvec, w_ada, b_ada).reshape(DEPTH * N_MOD_ROWS, 1, N_MOD)
    bias_tbl = _na_bias_table(rel_bias)
    consts_p = _dft_consts(SEQ)
    consts_s = _dft_consts(DEC_SEQ)
    u_bf = peer_u.astype(BF16)
    vt_bf = jnp.swapaxes(peer_v, 1, 2).astype(BF16)

    gfin = g_final.reshape(1, D_MODEL)
    kv = None
    for l in range(DEPTH):
        ga = g_attn[l].reshape(1, D_MODEL)
        gf = g_ffn[l].reshape(1, D_MODEL)
        wpg_bd = _block_diag(w_pool_grp[l]).astype(BF16)
        ps = pool_scale[l].reshape(1, D_POOL)
        z = _front(x, mods, l, ga, w_in[l].astype(BF16))
        bins_p, *kv = _mix_prompt(z, consts_p, wpg_bd, ps, kv)
        bins_s = _mix_sample(z, cache_k, cache_v, bias_tbl, l, consts_s, wpg_bd, ps)
        x, h2t = _merge(x, bins_p, bins_s, mods, l, ga, gf, w_gate[l].astype(BF16), b_gate[l].reshape(1, -1),
                        w_fourier[l].astype(BF16), w_attn[l].astype(BF16), w_pool[l].astype(BF16),
                        w_out[l].astype(BF16))
        tables = _peer_front(h2t, *_peer_query_weights(peer_wq[l], peer_keys[l]))
        x = _peer_dense(h2t, u_bf, vt_bf, tables, x, mods, l, gfin, final=(l == DEPTH - 1))

    y_prompt = x[:T_PROMPT].reshape(BATCH, SEQ, D_MODEL)
    y_sample = x[T_PROMPT:].reshape(DEC_BATCH, DEC_SEQ, D_MODEL)
    return y_prompt, y_sample, kv[0], kv[1]
```
